```python
import math
import jax, jax.numpy as jnp
from jax import lax
import numpy as np

D_MODEL = 2048
BATCH = 8
SEQ = 2048
DEPTH = 1

CTX_LEN = 256
GRID_W = 64
MIX_W = D_MODEL
HY_W = MIX_W // 2
HY_HEADS = 8
S5_W = MIX_W - HY_W
S5_GROUP = 16
S5_GROUPS = S5_W // S5_GROUP
S5_STATE = 64
HY_ORDER = 2
HY_BANDS = 16
HY_EMB = 1 + 2 * HY_BANDS
HY_FFN = 64
HY_DECAY_TARGET = 1e-2
HY_DECAY_FAST = 0.3
HY_DECAY_SLOW = 1.5
SHORT_K = 3
D_FF = -(-8 * D_MODEL // (3 * 256)) * 256
EPS = 1e-6

kernel_name = 'hymba_style_hyena_s5_dit_block'


def rmsnorm(x, g):
    xf = x.astype(jnp.float32)
    y = xf * lax.rsqrt(jnp.mean(xf * xf, axis=-1, keepdims=True) + EPS)
    return (y * g.astype(jnp.float32)).astype(x.dtype)


def modulate(h, shift, scale):
    return h * (1.0 + scale) + shift


def short_conv(u, w, b):
    n = u.shape[-2]
    up = jnp.pad(u, [(0, 0)] * (u.ndim - 2) + [(1, 1), (0, 0)])
    return up[..., :n, :] * w[0] + up[..., 1:n + 1, :] * w[1] + up[..., 2:, :] * w[2] + b


def hyena_filters(n, w1, b1, w2, b2, w3, sin_freq, decay):
    pos = jnp.arange(n, dtype=jnp.float32)
    t = pos[:, None] / n
    bands = jnp.linspace(1e-4, HY_BANDS - 1, HY_BANDS, dtype=jnp.float32)
    ang = 2.0 * math.pi * pos[:, None] * bands[None, :] / n
    z = jnp.concatenate([t, jnp.cos(ang), -jnp.sin(ang)], axis=-1)
    h = jnp.sin(sin_freq[0] * (z @ w1 + b1))
    h = jnp.sin(sin_freq[1] * (h @ w2 + b2))
    h = (h @ w3).astype(jnp.float32).reshape(n, HY_ORDER, 2, HY_W)
    h = h * jnp.exp(-t[:, :, None, None] * jnp.abs(decay.astype(jnp.float32)))
    fwd, bwd = h[:, :, 0], h[:, :, 1]
    two = jnp.concatenate([fwd, jnp.zeros((1, HY_ORDER, HY_W), jnp.float32), bwd[1:][::-1]], axis=0)
    return two / (jnp.sum(jnp.abs(two), axis=0, keepdims=True) + EPS)


def fftconv(u, h):
    n = u.shape[1]
    uf = jnp.fft.rfft(u.astype(jnp.float32), n=2 * n, axis=1)
    hf = jnp.fft.rfft(h, n=2 * n, axis=0)
    y = jnp.fft.irfft(uf * hf[None], n=2 * n, axis=1)[:, :n]
    return y.astype(u.dtype)


def hyena_mixer(hy, filt, conv_w, conv_b, bias, rows):
    bsz, n, ch = hy.shape
    if rows is None:
        hs = short_conv(hy, conv_w, conv_b)
    else:
        hs = short_conv(hy.reshape(bsz, rows, GRID_W, ch), conv_w, conv_b).reshape(bsz, n, ch)
    v, x1, x2 = jnp.split(hs, 3, axis=-1)
    z = v
    for o, gate in enumerate((x1, x2)):
        z = gate * (fftconv(z, filt[:, o]) + bias[o] * z)
    return z


def _lin_rec(e1, e2):
    a1, b1 = e1
    a2, b2 = e2
    return a1 * a2, a2 * b1 + b2


def s5_direction(ug, lam_re, lam_im, log_dt, b_re, b_im, x0, reverse):
    lam = lax.complex(lam_re.astype(jnp.float32), lam_im.astype(jnp.float32))
    lam_dt = lam * jnp.exp(log_dt.astype(jnp.float32))[:, None]
    a_bar = jnp.exp(lam_dt)
    b_bar = ((a_bar - 1.0) / lam)[..., None] * lax.complex(b_re.astype(jnp.float32), b_im.astype(jnp.float32))
    bu = jnp.einsum('gpc,blgc->blgp', b_bar, ug.astype(jnp.complex64))
    n = ug.shape[1]
    a = jnp.broadcast_to(a_bar, (1, n) + a_bar.shape)
    _, xs = lax.associative_scan(_lin_rec, (a, bu), axis=1, reverse=reverse)
    if x0 is not None:
        steps = jnp.arange(n, dtype=jnp.float32)
        k = (n - steps) if reverse else (steps + 1.0)
        xs = xs + jnp.exp(lam_dt[None] * k[:, None, None])[None] * x0[:, None]
    final = xs[:, 0] if reverse else xs[:, -1]
    return xs, final


def s5_readout(ug, xs_f, xs_b, c_re, c_im, d, glu_w, glu_b, out_dtype):
    cf = lax.complex(c_re[0].astype(jnp.float32), c_im[0].astype(jnp.float32))
    cb = lax.complex(c_re[1].astype(jnp.float32), c_im[1].astype(jnp.float32))
    y = (jnp.einsum('gcp,blgp->blgc', cf, xs_f).real
         + jnp.einsum('gcp,blgp->blgc', cb, xs_b).real
         + d.astype(jnp.float32) * ug)
    bsz, n = ug.shape[:2]
    y = jax.nn.gelu(y).reshape(bsz, n, S5_W).astype(out_dtype)
    gl = y @ glu_w + glu_b
    return gl[..., :S5_W] * jax.nn.sigmoid(gl[..., S5_W:])


def swiglu(h, wg, wu, wd):
    return (jax.nn.silu(h @ wg) * (h @ wu)) @ wd


def setup_inputs(seed: int = 0) -> dict:
    key = jax.random.key(seed)
    ks = iter(jax.random.split(key, 48))
    f32 = jnp.float32

    def nrm(shape, scale):
        return scale * jax.random.normal(next(ks), shape, f32)

    def gain(shape):
        return 1.0 + nrm(shape, 0.02)

    L, D, G, P = DEPTH, D_MODEL, S5_GROUPS, S5_STATE
    x = nrm((BATCH, SEQ, D), 1.0)
    c = nrm((BATCH, D), 1.0)
    ctx = nrm((BATCH, CTX_LEN, D), 1.0)
    c_ctx = nrm((D,), 1.0)
    ada_w = nrm((L, D, 6 * D), 0.5 * D ** -0.5)
    ada_b = nrm((L, 6 * D), 0.01)
    norm1_g = gain((L, D))
    w_in = nrm((L, D, S5_W + 3 * HY_W), D ** -0.5)
    conv_w = nrm((L, SHORT_K, 3 * HY_W), SHORT_K ** -0.5)
    conv_b = nrm((L, 3 * HY_W), 0.01)
    hy_w1 = nrm((L, HY_EMB, HY_FFN), HY_EMB ** -0.5)
    hy_b1 = nrm((L, HY_FFN), 0.02)
    hy_w2 = nrm((L, HY_FFN, HY_FFN), HY_FFN ** -0.5)
    hy_b2 = nrm((L, HY_FFN), 0.02)
    hy_w3 = nrm((L, HY_FFN, HY_ORDER * 2 * HY_W), HY_FFN ** -0.5)
    hy_sin_freq = 1.0 + nrm((L, 2, HY_FFN), 0.1)
    decay_base = jnp.abs(jnp.linspace(math.log(HY_DECAY_TARGET) / HY_DECAY_FAST,
                                      math.log(HY_DECAY_TARGET) / HY_DECAY_SLOW, HY_W, dtype=f32))
    hy_decay = decay_base * (1.0 + nrm((L, HY_ORDER, 2, HY_W), 0.05))
    hy_bias = nrm((L, HY_ORDER, HY_W), 0.5)
    s5_lam_re = -0.5 + nrm((L, 2, G, P), 0.01)
    s5_lam_im = jnp.broadcast_to(math.pi * jnp.arange(P, dtype=f32), (L, 2, G, P)) + nrm((L, 2, G, P), 0.001)
    s5_log_dt = jax.random.uniform(next(ks), (L, 2, G), f32, math.log(1e-3), math.log(1e-1))
    s5_b_re = nrm((L, 2, G, P, S5_GROUP), (2 * S5_GROUP) ** -0.5)
    s5_b_im = nrm((L, 2, G, P, S5_GROUP), (2 * S5_GROUP) ** -0.5)
    s5_c_re = nrm((L, 2, G, S5_GROUP, P), (2 * P) ** -0.5)
    s5_c_im = nrm((L, 2, G, S5_GROUP, P), (2 * P) ** -0.5)
    s5_d = nrm((L, G, S5_GROUP), 1.0)
    s5_glu_w = nrm((L, S5_W, 2 * S5_W), S5_W ** -0.5)
    s5_glu_b = nrm((L, 2 * S5_W), 0.01)
    branch_g_s5 = gain((L, S5_W))
    branch_g_hy = gain((L, HY_W))
    w_out = nrm((L, MIX_W, D), MIX_W ** -0.5)
    norm2_g = gain((L, D))
    ffn_w_gate = nrm((L, D, D_FF), D ** -0.5)
    ffn_w_up = nrm((L, D, D_FF), D ** -0.5)
    ffn_w_down = nrm((L, D_FF, D), D_FF ** -0.5)
    final_g = gain((D,))
    return {'x': x, 'c': c, 'ctx': ctx, 'c_ctx': c_ctx, 'ada_w': ada_w, 'ada_b': ada_b,
            'norm1_g': norm1_g, 'w_in': w_in, 'conv_w': conv_w, 'conv_b': conv_b,
            'hy_w1': hy_w1, 'hy_b1': hy_b1, 'hy_w2': hy_w2, 'hy_b2': hy_b2, 'hy_w3': hy_w3,
            'hy_sin_freq': hy_sin_freq, 'hy_decay': hy_decay, 'hy_bias': hy_bias,
            's5_lam_re': s5_lam_re, 's5_lam_im': s5_lam_im, 's5_log_dt': s5_log_dt,
            's5_b_re': s5_b_re, 's5_b_im': s5_b_im, 's5_c_re': s5_c_re, 's5_c_im': s5_c_im,
            's5_d': s5_d, 's5_glu_w': s5_glu_w, 's5_glu_b': s5_glu_b,
            'branch_g_s5': branch_g_s5, 'branch_g_hy': branch_g_hy, 'w_out': w_out,
            'norm2_g': norm2_g, 'ffn_w_gate': ffn_w_gate, 'ffn_w_up': ffn_w_up,
            'ffn_w_down': ffn_w_down, 'final_g': final_g}


def reference(x, c, ctx, c_ctx, ada_w, ada_b, norm1_g, w_in, conv_w, conv_b,
              hy_w1, hy_b1, hy_w2, hy_b2, hy_w3, hy_sin_freq, hy_decay, hy_bias,
              s5_lam_re, s5_lam_im, s5_log_dt, s5_b_re, s5_b_im, s5_c_re, s5_c_im,
              s5_d, s5_glu_w, s5_glu_b, branch_g_s5, branch_g_hy, w_out,
              norm2_g, ffn_w_gate, ffn_w_up, ffn_w_down, final_g):
    bsz, n_lat, _ = x.shape
    n_ctx = ctx.shape[1]
    rows = n_lat // GRID_W
    for l in range(DEPTH):
        last = l == DEPTH - 1
        mod = jax.nn.silu(c) @ ada_w[l] + ada_b[l]
        mod_c = jax.nn.silu(c_ctx) @ ada_w[l] + ada_b[l]
        sh1, sc1, g1, sh2, sc2, g2 = jnp.split(mod[:, None, :], 6, axis=-1)
        csh1, csc1, cg1, csh2, csc2, cg2 = jnp.split(mod_c, 6, axis=-1)
        s5_fwd = (s5_lam_re[l, 0], s5_lam_im[l, 0], s5_log_dt[l, 0], s5_b_re[l, 0], s5_b_im[l, 0])
        s5_bwd = (s5_lam_re[l, 1], s5_lam_im[l, 1], s5_log_dt[l, 1], s5_b_re[l, 1], s5_b_im[l, 1])

        hc = modulate(rmsnorm(ctx, norm1_g[l]), csh1, csc1)
        pc = hc @ (w_in[l, :, :S5_W] if last else w_in[l])
        ugc = pc[..., :S5_W].astype(jnp.float32).reshape(bsz, n_ctx, S5_GROUPS, S5_GROUP)
        xs_cf, fin_f = s5_direction(ugc, *s5_fwd, None, False)
        xs_cb, fin_b = s5_direction(ugc, *s5_bwd, None, True)

        h = modulate(rmsnorm(x, norm1_g[l]), sh1, sc1)
        p = h @ w_in[l]
        ug = p[..., :S5_W].astype(jnp.float32).reshape(bsz, n_lat, S5_GROUPS, S5_GROUP)
        xs_f, _ = s5_direction(ug, *s5_fwd, fin_f, False)
        xs_b, _ = s5_direction(ug, *s5_bwd, fin_b, True)
        y_s5 = s5_readout(ug, xs_f, xs_b, s5_c_re[l], s5_c_im[l], s5_d[l], s5_glu_w[l], s5_glu_b[l], x.dtype)
        filt = hyena_filters(n_lat, hy_w1[l], hy_b1[l], hy_w2[l], hy_b2[l], hy_w3[l], hy_sin_freq[l], hy_decay[l])
        y_hy = hyena_mixer(p[..., S5_W:], filt, conv_w[l], conv_b[l], hy_bias[l], rows)
        mix = jnp.concatenate([rmsnorm(y_s5, branch_g_s5[l]), rmsnorm(y_hy, branch_g_hy[l])], axis=-1)
        x = x + g1 * (mix @ w_out[l])

        h2 = modulate(rmsnorm(x, norm2_g[l]), sh2, sc2)
        x = x + g2 * swiglu(h2, ffn_w_gate[l], ffn_w_up[l], ffn_w_down[l])

        if not last:
            yc_s5 = s5_readout(ugc, xs_cf, xs_cb, s5_c_re[l], s5_c_im[l], s5_d[l], s5_glu_w[l], s5_glu_b[l], ctx.dtype)
            filt_c = hyena_filters(n_ctx, hy_w1[l], hy_b1[l], hy_w2[l], hy_b2[l], hy_w3[l], hy_sin_freq[l], hy_decay[l])
            yc_hy = hyena_mixer(pc[..., S5_W:], filt_c, conv_w[l], conv_b[l], hy_bias[l], None)
            mixc = jnp.concatenate([rmsnorm(yc_s5, branch_g_s5[l]), rmsnorm(yc_hy, branch_g_hy[l])], axis=-1)
            ctx = ctx + cg1 * (mixc @ w_out[l])
            hc2 = modulate(rmsnorm(ctx, norm2_g[l]), csh2, csc2)
            ctx = ctx + cg2 * swiglu(hc2, ffn_w_gate[l], ffn_w_up[l], ffn_w_down[l])
    return rmsnorm(x, final_g)
```

```python
import functools
import math

import numpy as np
import jax
import jax.numpy as jnp
from jax import lax
from jax.experimental import pallas as pl
from jax.experimental.pallas import tpu as pltpu

EPS = 1e-6
GRID_W = 64
S5_GROUP = 16
S5_STATE = 64
S5_CHUNK = 16
HY_ORDER = 2
HY_BANDS = 16
V7X_VMEM_BYTES = 64 * 1024 * 1024
VMEM_LIMIT = 56 * 1024 * 1024

F32 = jnp.float32
BF16 = jnp.bfloat16
HIGHEST = lax.Precision.HIGHEST


def _params(sem):
    return pltpu.CompilerParams(dimension_semantics=sem, vmem_limit_bytes=VMEM_LIMIT)


def _rms(x, g):
    return x * lax.rsqrt(jnp.mean(x * x, axis=-1, keepdims=True) + EPS) * g


def _ada_kernel(c_ref, w_ref, b_ref, o_ref):
    cv = c_ref[...]
    s = cv * jax.nn.sigmoid(cv)
    o_ref[...] = jnp.dot(s, w_ref[...], precision=HIGHEST, preferred_element_type=F32) + b_ref[...]


def _ada(cc, w, b, tn=1024):
    r, d = cc.shape
    n = w.shape[1]
    return pl.pallas_call(
        _ada_kernel,
        grid=(n // tn,),
        in_specs=[pl.BlockSpec((r, d), lambda j: (0, 0)),
                  pl.BlockSpec((d, tn), lambda j: (0, j)),
                  pl.BlockSpec((1, tn), lambda j: (0, j))],
        out_specs=pl.BlockSpec((r, tn), lambda j: (0, j)),
        out_shape=jax.ShapeDtypeStruct((r, n), F32),
        compiler_params=_params(("arbitrary",)),
        name="ada",
    )(cc, w, b.reshape(1, n))


def _inproj_kernel(x_ref, g_ref, sh_ref, sc_ref, w_ref, cw_ref, cb_ref, o_ref, h_scr, *, n_plain, tm):
    j = pl.program_id(2)

    @pl.when(j == 0)
    def _():
        h = _rms(x_ref[0], g_ref[...]) * (1.0 + sc_ref[0]) + sh_ref[0]
        h_scr[...] = h.astype(BF16)

    acc = jnp.dot(h_scr[...], w_ref[...], preferred_element_type=F32)

    @pl.when(j < n_plain)
    def _():
        o_ref[0] = acc.astype(o_ref.dtype)

    @pl.when(j >= n_plain)
    def _():
        col = lax.broadcasted_iota(jnp.int32, acc.shape, 0) % GRID_W
        prev = jnp.where(col == 0, 0.0, pltpu.roll(acc, 1, axis=0))
        nxt = jnp.where(col == GRID_W - 1, 0.0, pltpu.roll(acc, tm - 1, axis=0))
        cw = cw_ref[...]
        o_ref[0] = (prev * cw[0:1] + acc * cw[1:2] + nxt * cw[2:3] + cb_ref[...]).astype(o_ref.dtype)


def _inproj(x, g, sh, sc, w, cw, cb, n_plain_cols, tm, tn):
    bsz, n, d = x.shape
    nn = w.shape[1]
    n_plain = n_plain_cols // tn
    n_conv = cw.shape[1] // tn
    assert tm % GRID_W == 0 and n % tm == 0 and nn % tn == 0 and n_plain_cols % tn == 0
    conv_idx = lambda b, i, j: (0, jnp.clip(j - n_plain, 0, n_conv - 1))
    return pl.pallas_call(
        functools.partial(_inproj_kernel, n_plain=n_plain, tm=tm),
        grid=(bsz, n // tm, nn // tn),
        in_specs=[pl.BlockSpec((1, tm, d), lambda b, i, j: (b, i, 0)),
                  pl.BlockSpec((1, d), lambda b, i, j: (0, 0)),
                  pl.BlockSpec((1, 1, d), lambda b, i, j: (b, 0, 0)),
                  pl.BlockSpec((1, 1, d), lambda b, i, j: (b, 0, 0)),
                  pl.BlockSpec((d, tn), lambda b, i, j: (0, j)),
                  pl.BlockSpec((3, tn), conv_idx),
                  pl.BlockSpec((1, tn), conv_idx)],
        out_specs=pl.BlockSpec((1, tm, tn), lambda b, i, j: (b, i, j)),
        out_shape=jax.ShapeDtypeStruct((bsz, n, nn), BF16),
        scratch_shapes=[pltpu.VMEM((tm, d), BF16)],
        compiler_params=_params(("parallel", "parallel", "arbitrary")),
        name="inproj",
    )(x, g.reshape(1, d), sh, sc, w, cw, cb.reshape(1, -1))


def _s5_prep(lam_re, lam_im, log_dt, b_re, b_im, c_re, c_im, d):
    t = S5_CHUNK
    lam = lax.complex(lam_re.astype(F32), lam_im.astype(F32))
    lam_dt = lam * jnp.exp(log_dt.astype(F32))[..., None]
    a_bar = jnp.exp(lam_dt)
    b_bar = ((a_bar - 1.0) / lam)[..., None] * lax.complex(b_re.astype(F32), b_im.astype(F32))
    cc = lax.complex(c_re.astype(F32), c_im.astype(F32))
    k = jnp.arange(t + 1, dtype=F32)
    apow = jnp.exp(lam_dt[None] * k[:, None, None, None])
    g = lam.shape[1]

    wbf = apow[t - 1::-1, 0].transpose(1, 0, 2)[:, :, None, :] * b_bar[0].transpose(0, 2, 1)[:, None]
    wbb = apow[:t, 1].transpose(1, 0, 2)[:, :, None, :] * b_bar[1].transpose(0, 2, 1)[:, None]
    wb = jnp.concatenate([wbf.real, wbb.real, wbf.imag, wbb.imag], axis=-1).reshape(g, t * S5_GROUP, 4 * S5_STATE)

    wcf = cc[0].transpose(0, 2, 1)[:, :, None, :] * apow[1:, 0].transpose(1, 2, 0)[..., None]
    wcb = cc[1].transpose(0, 2, 1)[:, :, None, :] * apow[t:0:-1, 1].transpose(1, 2, 0)[..., None]
    wc = jnp.concatenate([wcf.real, wcb.real, -wcf.imag, -wcb.imag], axis=1).reshape(g, 4 * S5_STATE, t * S5_GROUP)

    def lag_kernel(dr):
        kk = jnp.einsum('gcp,kgp,gpe->kgec', cc[dr], apow[:t, dr], b_bar[dr], precision=HIGHEST)
        return kk.real

    kf, kb = lag_kernel(0), lag_kernel(1)
    sig = jnp.arange(t)[:, None]
    tau = jnp.arange(t)[None, :]
    lag = tau - sig
    mf = jnp.where((lag >= 0)[:, :, None, None, None], kf[jnp.clip(lag, 0, t - 1)], 0.0)
    mb = jnp.where((lag <= 0)[:, :, None, None, None], kb[jnp.clip(-lag, 0, t - 1)], 0.0)
    skip = (lag == 0)[:, :, None, None, None] * (jnp.eye(S5_GROUP, dtype=F32) * d.astype(F32)[:, None, :])[None, None]
    m = (mf + mb + skip).transpose(2, 0, 3, 1, 4).reshape(g, t * S5_GROUP, t * S5_GROUP)

    a16 = apow[t]
    ar = jnp.concatenate([a16[0].real, a16[1].real], axis=-1)[:, None, :]
    ai = jnp.concatenate([a16[0].imag, a16[1].imag], axis=-1)[:, None, :]
    return wb.astype(BF16), wc.astype(BF16), m.astype(BF16), ar, ai


def _gelu_tanh(x):
    return 0.5 * x * (1.0 + jnp.tanh(math.sqrt(2.0 / math.pi) * (x + 0.044715 * (x * x * x))))


def _s5_kernel(u_ref, wb_ref, wc_ref, m_ref, ar_ref, ai_ref, o_ref,
               p_scr, fw_re, fw_im, bw_re, bw_im, *, n_ctx, n_lat, bsz):
    half = 2 * S5_STATE
    u = u_ref[0]
    p_scr[...] = jnp.dot(u, wb_ref[0], preferred_element_type=F32)
    ar = jnp.broadcast_to(ar_ref[0], (bsz, half))
    ai = jnp.broadcast_to(ai_ref[0], (bsz, half))
    is_fwd = lax.broadcasted_iota(jnp.int32, (bsz, half), 1) < S5_STATE

    def rows(i):
        return pl.ds(pl.multiple_of(i * bsz, bsz), bsz)

    def advance(s_re, s_im, fi, bi):
        in_re = jnp.where(is_fwd, p_scr[rows(fi), 0:half], p_scr[rows(bi), 0:half])
        in_im = jnp.where(is_fwd, p_scr[rows(fi), half:2 * half], p_scr[rows(bi), half:2 * half])
        return ar * s_re - ai * s_im + in_re, ar * s_im + ai * s_re + in_im

    def ctx_step(k, carry):
        return advance(carry[0], carry[1], k, n_ctx - 1 - k)

    def lat_step(k, carry):
        s_re, s_im = carry
        kb = n_lat - 1 - k
        fw_re[rows(k), :] = s_re
        fw_im[rows(k), :] = s_im
        bw_re[rows(kb), :] = s_re
        bw_im[rows(kb), :] = s_im
        return advance(s_re, s_im, n_ctx + k, n_ctx + kb)

    zero = jnp.zeros((bsz, half), F32)
    carry = lax.fori_loop(0, n_ctx, ctx_step, (zero, zero))
    lax.fori_loop(0, n_lat, lat_step, carry)

    sel = lax.broadcasted_iota(jnp.int32, (n_lat * bsz, half), 1) < S5_STATE
    ent = jnp.concatenate([jnp.where(sel, fw_re[...], bw_re[...]),
                           jnp.where(sel, fw_im[...], bw_im[...])], axis=1).astype(BF16)
    u_lat = u_ref[0, n_ctx * bsz:, :]
    y = (jnp.dot(u_lat, m_ref[0], preferred_element_type=F32)
         + jnp.dot(ent, wc_ref[0], preferred_element_type=F32))
    o_ref[0] = _gelu_tanh(y).astype(o_ref.dtype)


def _s5(u_all, wb, wc, m, ar, ai, n_ctx, n_lat, bsz):
    g, r_all, w = u_all.shape
    r_lat = n_lat * bsz
    half = 2 * S5_STATE
    blk = lambda shape: pl.BlockSpec((1,) + shape, lambda i: (i, 0, 0))
    return pl.pallas_call(
        functools.partial(_s5_kernel, n_ctx=n_ctx, n_lat=n_lat, bsz=bsz),
        grid=(g,),
        in_specs=[blk((r_all, w)), blk((w, 2 * half)), blk((2 * half, w)), blk((w, w)),
                  blk((1, half)), blk((1, half))],
        out_specs=blk((r_lat, w)),
        out_shape=jax.ShapeDtypeStruct((g, r_lat, w), BF16),
        scratch_shapes=[pltpu.VMEM((r_all, 2 * half), F32)] + [pltpu.VMEM((r_lat, half), F32)] * 4,
        compiler_params=_params(("parallel",)),
        name="s5",
    )(u_all, wb, wc, m, ar, ai)


def _filt_kernel(z_ref, w1_ref, b1_ref, w2_ref, b2_ref, w3_ref, fr_ref, dec_ref, o_ref):
    z = z_ref[...]
    fr = fr_ref[...]
    h = jnp.sin(fr[0:1] * (jnp.dot(z, w1_ref[...], precision=HIGHEST, preferred_element_type=F32) + b1_ref[...]))
    h = jnp.sin(fr[1:2] * (jnp.dot(h, w2_ref[...], precision=HIGHEST, preferred_element_type=F32) + b2_ref[...]))
    h3 = jnp.dot(h, w3_ref[...], precision=HIGHEST, preferred_element_type=F32)
    o_ref[...] = h3 * jnp.exp(-z[:, 0:1] * jnp.abs(dec_ref[...]))


def _filters(z, w1, b1, w2, b2, w3, freq, decay, tn=512):
    n, e = z.shape
    f = w1.shape[1]
    nn = w3.shape[1]
    full = lambda shape: pl.BlockSpec(shape, lambda j: (0, 0))
    return pl.pallas_call(
        _filt_kernel,
        grid=(nn // tn,),
        in_specs=[full((n, e)), full((e, f)), full((1, f)), full((f, f)), full((1, f)),
                  pl.BlockSpec((f, tn), lambda j: (0, j)), full((2, f)),
                  pl.BlockSpec((1, tn), lambda j: (0, j))],
        out_specs=pl.BlockSpec((n, tn), lambda j: (0, j)),
        out_shape=jax.ShapeDtypeStruct((n, nn), F32),
        compiler_params=_params(("parallel",)),
        name="filt",
    )(z, w1, b1.reshape(1, f), w2, b2.reshape(1, f), w3, freq, decay.reshape(1, nn))


def _dft_tables(n):
    k = np.arange(n, dtype=np.int64)
    ang = (np.outer(k, k) % (2 * n)).astype(np.float64) * (math.pi / n)
    return jnp.asarray(np.cos(ang), dtype=BF16), jnp.asarray(-np.sin(ang), dtype=BF16)


def _alt(n):
    return jnp.where(lax.broadcasted_iota(jnp.int32, (n, 1), 0) % 2 == 0, 1.0, -1.0).astype(F32)


def _spec_kernel(ff_ref, fb_ref, c_ref, s_ref, hr_ref, hi_ref, hn_ref):
    ff = ff_ref[...]
    fb = fb_ref[...]
    n = ff.shape[0]
    b0 = fb[0:1]
    den = (jnp.sum(jnp.abs(ff), axis=0, keepdims=True) + jnp.sum(jnp.abs(fb), axis=0, keepdims=True)
           - jnp.abs(b0) + EPS)
    inv = 1.0 / den
    ffb = ff.astype(BF16)
    fbb = fb.astype(BF16)
    cmat = c_ref[...]
    smat = s_ref[...]
    re = (jnp.dot(cmat, ffb, preferred_element_type=F32) + jnp.dot(cmat, fbb, preferred_element_type=F32) - b0)
    im = (jnp.dot(smat, ffb, preferred_element_type=F32) - jnp.dot(smat, fbb, preferred_element_type=F32))
    wk = jnp.where(lax.broadcasted_iota(jnp.int32, (n, 1), 0) == 0, 0.5 / n, 1.0 / n)
    hr_ref[0] = re * (wk * inv)
    hi_ref[0] = im * (wk * inv)
    alt = _alt(n)
    hn_ref[0] = (jnp.sum(ff * alt, axis=0, keepdims=True) + jnp.sum(fb * alt, axis=0, keepdims=True) - b0) * (inv * (0.5 / n))


def _spectrum(filt, cmat, smat, hy_w, tn=256):
    n = filt.shape[0]
    nt = hy_w // tn
    const = lambda shape: pl.BlockSpec(shape, lambda o, j: (0, 0), pipeline_mode=pl.Buffered(1))
    out_blk = lambda r: pl.BlockSpec((1, r, tn), lambda o, j: (o, 0, j))
    return pl.pallas_call(
        _spec_kernel,
        grid=(HY_ORDER, nt),
        in_specs=[pl.BlockSpec((n, tn), lambda o, j: (0, o * 2 * nt + j)),
                  pl.BlockSpec((n, tn), lambda o, j: (0, o * 2 * nt + nt + j)),
                  const((n, n)), const((n, n))],
        out_specs=[out_blk(n), out_blk(n), out_blk(1)],
        out_shape=[jax.ShapeDtypeStruct((HY_ORDER, n, hy_w), F32),
                   jax.ShapeDtypeStruct((HY_ORDER, n, hy_w), F32),
                   jax.ShapeDtypeStruct((HY_ORDER, 1, hy_w), F32)],
        compiler_params=_params(("parallel", "parallel")),
        name="spec",
    )(filt, filt, cmat, smat)


def _hconv_kernel(z_ref, gate_ref, bias_ref, hr_ref, hi_ref, hn_ref, c_ref, s_ref, o_ref, y_scr, *, fb):
    z = z_ref[0]
    n = z.shape[0]
    zf = z.astype(F32)
    alt = _alt(n)
    z_nyq = jnp.sum(zf * alt, axis=0, keepdims=True)
    y_scr[...] = alt * (z_nyq * hn_ref[0]) + bias_ref[0] * zf

    def body(i, carry):
        r0 = pl.multiple_of(i * fb, fb)
        zr = jnp.dot(c_ref[pl.ds(r0, fb), :], z, preferred_element_type=F32)
        zi = jnp.dot(s_ref[pl.ds(r0, fb), :], z, preferred_element_type=F32)
        hr = hr_ref[0, pl.ds(r0, fb), :]
        hi = hi_ref[0, pl.ds(r0, fb), :]
        yr = (zr * hr - zi * hi).astype(BF16)
        yi = (zr * hi + zi * hr).astype(BF16)
        y_scr[...] += (jnp.dot(c_ref[:, pl.ds(r0, fb)], yr, preferred_element_type=F32)
                       + jnp.dot(s_ref[:, pl.ds(r0, fb)], yi, preferred_element_type=F32))
        return carry

    lax.fori_loop(0, n // fb, body, 0)
    o_ref[0] = (gate_ref[0].astype(F32) * y_scr[...]).astype(o_ref.dtype)


def _hconv(zsrc, z_off, gsrc, g_off, bias, hr, hi, hn, order, cmat, smat, hy_w, tc=256, fb=256):
    bsz, n, _ = zsrc.shape
    nt = hy_w // tc
    zo, go = z_off // tc, g_off // tc
    const = lambda shape: pl.BlockSpec(shape, lambda j, b: (0, 0), pipeline_mode=pl.Buffered(1))
    return pl.pallas_call(
        functools.partial(_hconv_kernel, fb=fb),
        grid=(nt, bsz),
        in_specs=[pl.BlockSpec((1, n, tc), lambda j, b: (b, 0, zo + j)),
                  pl.BlockSpec((1, n, tc), lambda j, b: (b, 0, go + j)),
                  pl.BlockSpec((1, 1, tc), lambda j, b: (order, 0, j)),
                  pl.BlockSpec((1, n, tc), lambda j, b: (order, 0, j)),
                  pl.BlockSpec((1, n, tc), lambda j, b: (order, 0, j)),
                  pl.BlockSpec((1, 1, tc), lambda j, b: (order, 0, j)),
                  const((n, n)), const((n, n))],
        out_specs=pl.BlockSpec((1, n, tc), lambda j, b: (b, 0, j)),
        out_shape=jax.ShapeDtypeStruct((bsz, n, hy_w), BF16),
        scratch_shapes=[pltpu.VMEM((n, tc), F32)],
        compiler_params=_params(("parallel", "parallel")),
        name="hconv",
    )(zsrc, gsrc, bias, hr, hi, hn, cmat, smat)


def _mix_kernel(ys_ref, yh_ref, x_ref, gw_ref, gb_ref, gs_ref, gh_ref, wo_ref, g1_ref, o_ref, *, s5_w):
    gl = jnp.dot(ys_ref[0], gw_ref[...], preferred_element_type=F32) + gb_ref[...]
    a = gl[:, :s5_w] * jax.nn.sigmoid(gl[:, s5_w:])
    a = _rms(a, gs_ref[...]).astype(BF16)
    yh = _rms(yh_ref[0].astype(F32), gh_ref[...]).astype(BF16)
    proj = (jnp.dot(a, wo_ref[:s5_w, :], preferred_element_type=F32)
            + jnp.dot(yh, wo_ref[s5_w:, :], preferred_element_type=F32))
    o_ref[0] = x_ref[0] + g1_ref[0] * proj


def _mix(ys, yh, x, glu_w, glu_b, g_s5, g_hy, w_out, g1, tm=512):
    bsz, n, d = x.shape
    tm = min(tm, n)
    s5_w = ys.shape[-1]
    hy_w = yh.shape[-1]
    const = lambda shape: pl.BlockSpec(shape, lambda b, i: (0, 0), pipeline_mode=pl.Buffered(1))
    tok = lambda w: pl.BlockSpec((1, tm, w), lambda b, i: (b, i, 0))
    return pl.pallas_call(
        functools.partial(_mix_kernel, s5_w=s5_w),
        grid=(bsz, n // tm),
        in_specs=[tok(s5_w), tok(hy_w), tok(d),
                  const((s5_w, 2 * s5_w)), const((1, 2 * s5_w)), const((1, s5_w)), const((1, hy_w)),
                  const((s5_w + hy_w, d)),
                  pl.BlockSpec((1, 1, d), lambda b, i: (b, 0, 0))],
        out_specs=tok(d),
        out_shape=jax.ShapeDtypeStruct((bsz, n, d), F32),
        compiler_params=_params(("parallel", "parallel")),
        name="mix",
    )(ys, yh, x, glu_w, glu_b.reshape(1, -1), g_s5.reshape(1, -1), g_hy.reshape(1, -1), w_out, g1)


def _ffn_kernel(x_ref, g_ref, sh_ref, sc_ref, wg_ref, wu_ref, wd_ref, g2_ref, fg_ref, o_ref, h_scr, acc_scr):
    j = pl.program_id(2)

    @pl.when(j == 0)
    def _():
        h = _rms(x_ref[0], g_ref[...]) * (1.0 + sc_ref[0]) + sh_ref[0]
        h_scr[...] = h.astype(BF16)
        acc_scr[...] = jnp.zeros_like(acc_scr)

    h = h_scr[...]
    gate = jnp.dot(h, wg_ref[...], preferred_element_type=F32)
    up = jnp.dot(h, wu_ref[...], preferred_element_type=F32)
    act = (gate * jax.nn.sigmoid(gate) * up).astype(BF16)
    acc_scr[...] += jnp.dot(act, wd_ref[...], preferred_element_type=F32)

    @pl.when(j == pl.num_programs(2) - 1)
    def _():
        o_ref[0] = _rms(x_ref[0] + g2_ref[0] * acc_scr[...], fg_ref[...])


def _ffn(x, g, sh, sc, wg, wu, wd, g2, fg, tm=512, tf=512):
    bsz, n, d = x.shape
    tm = min(tm, n)
    dff = wg.shape[1]
    row = lambda: pl.BlockSpec((1, d), lambda b, i, j: (0, 0))
    per_b = lambda: pl.BlockSpec((1, 1, d), lambda b, i, j: (b, 0, 0))
    return pl.pallas_call(
        _ffn_kernel,
        grid=(bsz, n // tm, dff // tf),
        in_specs=[pl.BlockSpec((1, tm, d), lambda b, i, j: (b, i, 0)),
                  row(), per_b(), per_b(),
                  pl.BlockSpec((d, tf), lambda b, i, j: (0, j)),
                  pl.BlockSpec((d, tf), lambda b, i, j: (0, j)),
                  pl.BlockSpec((tf, d), lambda b, i, j: (j, 0)),
                  per_b(), row()],
        out_specs=pl.BlockSpec((1, tm, d), lambda b, i, j: (b, i, 0)),
        out_shape=jax.ShapeDtypeStruct((bsz, n, d), F32),
        scratch_shapes=[pltpu.VMEM((tm, d), BF16), pltpu.VMEM((tm, d), F32)],
        compiler_params=_params(("parallel", "parallel", "arbitrary")),
        name="ffn",
    )(x, g.reshape(1, d), sh, sc, wg, wu, wd, g2, fg.reshape(1, d))


def _chunk_rows(u, groups):
    bsz, n, _ = u.shape
    u = u.reshape(bsz, n // S5_CHUNK, S5_CHUNK, groups, S5_GROUP)
    return u.transpose(3, 1, 0, 2, 4).reshape(groups, (n // S5_CHUNK) * bsz, S5_CHUNK * S5_GROUP)


def _unchunk_rows(y, bsz):
    groups, r, _ = y.shape
    nch = r // bsz
    y = y.reshape(groups, nch, bsz, S5_CHUNK, S5_GROUP)
    return y.transpose(2, 1, 3, 0, 4).reshape(bsz, nch * S5_CHUNK, groups * S5_GROUP)


def _positional_features(n):
    pos = jnp.arange(n, dtype=F32)
    t = pos[:, None] / n
    bands = jnp.linspace(1e-4, HY_BANDS - 1, HY_BANDS, dtype=F32)
    ang = 2.0 * math.pi * pos[:, None] * bands[None, :] / n
    return jnp.concatenate([t, jnp.cos(ang), -jnp.sin(ang)], axis=-1)


def kernel(x, c, ctx, c_ctx, ada_w, ada_b, norm1_g, w_in, conv_w, conv_b, hy_w1, hy_b1, hy_w2, hy_b2, hy_w3,
           hy_sin_freq, hy_decay, hy_bias, s5_lam_re, s5_lam_im, s5_log_dt, s5_b_re, s5_b_im, s5_c_re, s5_c_im,
           s5_d, s5_glu_w, s5_glu_b, branch_g_s5, branch_g_hy, w_out, norm2_g, ffn_w_gate, ffn_w_up,
           ffn_w_down, final_g):
    bsz, n_lat, d = x.shape
    n_ctx = ctx.shape[1]
    assert ada_w.shape[0] == 1, "single-layer block"
    l = 0
    s5_w = s5_glu_w.shape[1]
    hy_w = w_in.shape[2] - s5_w
    hy_w //= 3
    groups = s5_w // S5_GROUP
    assert n_lat % S5_CHUNK == 0 and n_ctx % S5_CHUNK == 0 and n_lat % GRID_W == 0

    pad_rows = -(bsz + 1) % 8
    cc = jnp.concatenate([c, c_ctx[None, :], jnp.zeros((pad_rows, d), F32)], axis=0)
    mod = _ada(cc, ada_w[l], ada_b[l])
    sh1, sc1, g1, sh2, sc2, g2 = [mod[:bsz, None, i * d:(i + 1) * d] for i in range(6)]
    csh1, csc1 = [jnp.broadcast_to(mod[bsz, None, None, i * d:(i + 1) * d], (bsz, 1, d)) for i in range(2)]

    w_in_b = w_in[l].astype(BF16)
    p = _inproj(x, norm1_g[l], sh1, sc1, w_in_b, conv_w[l], conv_b[l], s5_w, tm=min(512, n_lat), tn=512)
    pc = _inproj(ctx, norm1_g[l], csh1, csc1, w_in_b[:, :s5_w], conv_w[l], conv_b[l], s5_w, tm=n_ctx, tn=512)

    wb, wc, m, ar, ai = _s5_prep(s5_lam_re[l], s5_lam_im[l], s5_log_dt[l], s5_b_re[l], s5_b_im[l],
                                 s5_c_re[l], s5_c_im[l], s5_d[l])
    u_all = jnp.concatenate([_chunk_rows(pc, groups), _chunk_rows(p[..., :s5_w], groups)], axis=1)
    ys = _s5(u_all, wb, wc, m, ar, ai, n_ctx // S5_CHUNK, n_lat // S5_CHUNK, bsz)
    ys = _unchunk_rows(ys, bsz)

    z = _positional_features(n_lat)
    e_pad = -z.shape[1] % 128
    z = jnp.pad(z, ((0, 0), (0, e_pad)))
    w1 = jnp.pad(hy_w1[l], ((0, e_pad), (0, 0)))
    filt = _filters(z, w1, hy_b1[l], hy_w2[l], hy_b2[l], hy_w3[l], hy_sin_freq[l],
                    hy_decay[l].reshape(-1))
    cmat, smat = _dft_tables(n_lat)
    hr, hi, hn = _spectrum(filt, cmat, smat, hy_w)
    bias = hy_bias[l].reshape(HY_ORDER, 1, hy_w)
    z1 = _hconv(p, s5_w, p, s5_w + hy_w, bias, hr, hi, hn, 0, cmat, smat, hy_w)
    yh = _hconv(z1, 0, p, s5_w + 2 * hy_w, bias, hr, hi, hn, 1, cmat, smat, hy_w)

    x1 = _mix(ys, yh, x, s5_glu_w[l].astype(BF16), s5_glu_b[l], branch_g_s5[l], branch_g_hy[l],
              w_out[l].astype(BF16), g1)

    return _ffn(x1, norm2_g[l], sh2, sc2, ffn_w_gate[l].astype(BF16), ffn_w_up[l].astype(BF16),
                ffn_w_down[l].astype(BF16), g2, final_g)
```

```python
import functools
import math

import numpy as np
import jax
import jax.numpy as jnp
from jax import lax
from jax.experimental import pallas as pl
from jax.experimental.pallas import tpu as pltpu

EPS = 1e-6
GRID_W = 64
S5_GROUP = 16
S5_STATE = 64
S5_CHUNK = 16
HY_ORDER = 2
HY_BANDS = 16
V7X_VMEM_BYTES = 64 * 1024 * 1024
VMEM_LIMIT = 56 * 1024 * 1024

F32 = jnp.float32
BF16 = jnp.bfloat16
HIGHEST = lax.Precision.HIGHEST


def _params(sem):
    return pltpu.CompilerParams(dimension_semantics=sem, vmem_limit_bytes=VMEM_LIMIT)


def _rms(x, g):
    return x * lax.rsqrt(jnp.mean(x * x, axis=-1, keepdims=True) + EPS) * g


def _ada_kernel(c_ref, w_ref, b_ref, o_ref):
    cv = c_ref[...]
    s = cv * jax.nn.sigmoid(cv)
    o_ref[...] = jnp.dot(s, w_ref[...], precision=HIGHEST, preferred_element_type=F32) + b_ref[...]


def _ada(cc, w, b, tn=1024):
    r, d = cc.shape
    n = w.shape[1]
    return pl.pallas_call(
        _ada_kernel,
        grid=(n // tn,),
        in_specs=[pl.BlockSpec((r, d), lambda j: (0, 0)),
                  pl.BlockSpec((d, tn), lambda j: (0, j)),
                  pl.BlockSpec((1, tn), lambda j: (0, j))],
        out_specs=pl.BlockSpec((r, tn), lambda j: (0, j)),
        out_shape=jax.ShapeDtypeStruct((r, n), F32),
        compiler_params=_params(("arbitrary",)),
        name="ada",
    )(cc, w, b.reshape(1, n))


def _gran_transpose(a):
    r = lax.broadcasted_iota(jnp.int32, a.shape, 1)
    q = lax.broadcasted_iota(jnp.int32, a.shape, 2) // S5_GROUP
    for dist in (4, 2, 1):
        rb = (r & dist) != 0
        qb = (q & dist) != 0
        up = pltpu.roll(pltpu.roll(a, 8 - dist, axis=1), S5_GROUP * dist, axis=2)
        dn = pltpu.roll(pltpu.roll(a, dist, axis=1), 128 - S5_GROUP * dist, axis=2)
        a = jnp.where(rb == qb, a, jnp.where(rb, dn, up))
    return a


def _inproj_kernel(*refs, s5_cols, tn, conv):
    if conv:
        x_ref, g_ref, sh_ref, sc_ref, w_ref, cw_ref, cb_ref, ua_ref, ub_ref, o_ref, h_scr = refs
    else:
        x_ref, g_ref, sh_ref, sc_ref, w_ref, ua_ref, ub_ref, h_scr = refs
    tm = x_ref.shape[1]
    h = _rms(x_ref[0], g_ref[...]) * (1.0 + sc_ref[0]) + sh_ref[0]
    h_scr[...] = h.astype(BF16)

    for jn in range(s5_cols // tn):
        acc = jnp.dot(h_scr[...], w_ref[:, jn * tn:(jn + 1) * tn], preferred_element_type=F32)
        for cb in range(tn // 128):
            tile = acc[:, cb * 128:(cb + 1) * 128].reshape(tm // S5_CHUNK, 2, 8, 128)
            gb = jn * (tn // 128) + cb
            ua_ref[gb, :, 0] = _gran_transpose(tile[:, 0])
            ub_ref[gb, :, 0] = _gran_transpose(tile[:, 1])

    if conv:
        col = lax.broadcasted_iota(jnp.int32, (tm, tn), 0) % GRID_W
        for jn in range((w_ref.shape[1] - s5_cols) // tn):
            c0 = jn * tn
            acc = jnp.dot(h_scr[...], w_ref[:, s5_cols + c0:s5_cols + c0 + tn], preferred_element_type=F32)
            prev = jnp.where(col == 0, 0.0, pltpu.roll(acc, 1, axis=0))
            nxt = jnp.where(col == GRID_W - 1, 0.0, pltpu.roll(acc, tm - 1, axis=0))
            cw = cw_ref[:, c0:c0 + tn]
            o_ref[0, :, c0:c0 + tn] = (prev * cw[0:1] + acc * cw[1:2] + nxt * cw[2:3]
                                       + cb_ref[:, c0:c0 + tn]).astype(o_ref.dtype)


def _inproj(x, g, sh, sc, w, cw, cb, s5_cols, tm, tn=512):
    bsz, n, d = x.shape
    nn = w.shape[1]
    conv = cw is not None
    assert tm % GRID_W == 0 and n % tm == 0 and s5_cols % tn == 0 and (nn - s5_cols) % tn == 0
    ngb = s5_cols // 128
    const = lambda shape: pl.BlockSpec(shape, lambda b, i: (0,) * len(shape), pipeline_mode=pl.Buffered(1))
    per_b = pl.BlockSpec((1, 1, d), lambda b, i: (b, 0, 0))
    in_specs = [pl.BlockSpec((1, tm, d), lambda b, i: (b, i, 0)), const((1, d)), per_b, per_b, const((d, nn))]
    args = [x, g.reshape(1, d), sh, sc, w]
    u_spec = pl.BlockSpec((ngb, tm // S5_CHUNK, 1, 8, 128), lambda b, i: (0, i, b, 0, 0))
    u_shape = jax.ShapeDtypeStruct((ngb, n // S5_CHUNK, bsz, 8, 128), F32)
    out_specs, out_shape = [u_spec, u_spec], [u_shape, u_shape]
    if conv:
        in_specs += [const((3, nn - s5_cols)), const((1, nn - s5_cols))]
        args += [cw, cb.reshape(1, -1)]
        out_specs.append(pl.BlockSpec((1, tm, nn - s5_cols), lambda b, i: (b, i, 0)))
        out_shape.append(jax.ShapeDtypeStruct((bsz, n, nn - s5_cols), BF16))
    return pl.pallas_call(
        functools.partial(_inproj_kernel, s5_cols=s5_cols, tn=tn, conv=conv),
        grid=(bsz, n // tm),
        in_specs=in_specs,
        out_specs=out_specs,
        out_shape=out_shape,
        scratch_shapes=[pltpu.VMEM((tm, d), BF16)],
        compiler_params=_params(("parallel", "parallel")),
        name="inproj" if conv else "inproj_ctx",
    )(*args)


def _s5_prep(lam_re, lam_im, log_dt, b_re, b_im, c_re, c_im, d):
    t = S5_CHUNK
    lam = lax.complex(lam_re.astype(F32), lam_im.astype(F32))
    lam_dt = lam * jnp.exp(log_dt.astype(F32))[..., None]
    a_bar = jnp.exp(lam_dt)
    b_bar = ((a_bar - 1.0) / lam)[..., None] * lax.complex(b_re.astype(F32), b_im.astype(F32))
    cc = lax.complex(c_re.astype(F32), c_im.astype(F32))
    k = jnp.arange(t + 1, dtype=F32)
    apow = jnp.exp(lam_dt[None] * k[:, None, None, None])
    g = lam.shape[1]

    wbf = apow[t - 1::-1, 0].transpose(1, 0, 2)[:, :, None, :] * b_bar[0].transpose(0, 2, 1)[:, None]
    wbb = apow[:t, 1].transpose(1, 0, 2)[:, :, None, :] * b_bar[1].transpose(0, 2, 1)[:, None]
    wb = jnp.concatenate([wbf.real, wbb.real, wbf.imag, wbb.imag], axis=-1).reshape(g, t * S5_GROUP, 4 * S5_STATE)

    wcf = cc[0].transpose(0, 2, 1)[:, :, None, :] * apow[1:, 0].transpose(1, 2, 0)[..., None]
    wcb = cc[1].transpose(0, 2, 1)[:, :, None, :] * apow[t:0:-1, 1].transpose(1, 2, 0)[..., None]
    wc = jnp.concatenate([wcf.real, wcb.real, -wcf.imag, -wcb.imag], axis=1).reshape(g, 4 * S5_STATE, t * S5_GROUP)

    def lag_kernel(dr):
        kk = jnp.einsum('gcp,kgp,gpe->kgec', cc[dr], apow[:t, dr], b_bar[dr], precision=HIGHEST)
        return kk.real

    kf, kb = lag_kernel(0), lag_kernel(1)
    lead = ((0, 0),) * 3
    mf = jnp.stack([jnp.pad(kf[:t - s], ((s, 0),) + lead) for s in range(t)])
    mb = jnp.stack([jnp.pad(kb[:s + 1][::-1], ((0, t - 1 - s),) + lead) for s in range(t)])
    skip = jnp.eye(t, dtype=F32)[:, :, None, None, None] * (jnp.eye(S5_GROUP, dtype=F32) * d.astype(F32)[:, None, :])[None, None]
    m = (mf + mb + skip).transpose(2, 0, 3, 1, 4).reshape(g, t * S5_GROUP, t * S5_GROUP)

    a16 = apow[t]
    ar = jnp.concatenate([a16[0].real, a16[1].real], axis=-1)[:, None, :]
    ai = jnp.concatenate([a16[0].imag, a16[1].imag], axis=-1)[:, None, :]
    return wb.astype(BF16), wc.astype(BF16), m.astype(BF16), ar, ai


def _gelu_tanh(x):
    return 0.5 * x * (1.0 + jnp.tanh(math.sqrt(2.0 / math.pi) * (x + 0.044715 * (x * x * x))))


S5_GB = 8
S5_PASS = 4


def _s5_kernel(ua_ref, ub_ref, uca_ref, ucb_ref, wb_ref, wc_ref, m_ref, ar_ref, ai_ref, o_ref,
               p_re, p_im, fw_re, fw_im, bw_re, bw_im, y_a, y_b, *, n_ctx, n_lat, bsz):
    half = 2 * S5_STATE
    r_ctx, r_lat = n_ctx * bsz, n_lat * bsz
    is_fwd = lax.broadcasted_iota(jnp.int32, (bsz, half), 1) < S5_STATE
    sel = lax.broadcasted_iota(jnp.int32, (r_lat, half), 1) < S5_STATE

    def load_u(a_ref, b_ref, g8, r):
        return jnp.concatenate([a_ref[0, pl.ds(g8, r, stride=S5_GB), :],
                                b_ref[0, pl.ds(g8, r, stride=S5_GB), :]], axis=1).astype(BF16)

    def rows(i):
        return pl.ds(pl.multiple_of(i * bsz, bsz), bsz)

    def one_pass(pass_idx, _):
        first = pass_idx * S5_PASS
        for q in range(S5_PASS):
            wbg = wb_ref[first + q]
            pc = jnp.dot(load_u(uca_ref, ucb_ref, first + q, r_ctx), wbg, preferred_element_type=F32)
            pn = jnp.dot(load_u(ua_ref, ub_ref, first + q, r_lat), wbg, preferred_element_type=F32)
            p_re[q, 0:r_ctx, :] = pc[:, :half]
            p_im[q, 0:r_ctx, :] = pc[:, half:]
            p_re[q, r_ctx:, :] = pn[:, :half]
            p_im[q, r_ctx:, :] = pn[:, half:]

        ars = [jnp.broadcast_to(ar_ref[first + q], (bsz, half)) for q in range(S5_PASS)]
        ais = [jnp.broadcast_to(ai_ref[first + q], (bsz, half)) for q in range(S5_PASS)]

        def advance(carry, fi, bi):
            out = []
            for q in range(S5_PASS):
                s_re, s_im = carry[2 * q], carry[2 * q + 1]
                in_re = jnp.where(is_fwd, p_re[q, rows(fi), :], p_re[q, rows(bi), :])
                in_im = jnp.where(is_fwd, p_im[q, rows(fi), :], p_im[q, rows(bi), :])
                out += [ars[q] * s_re - ais[q] * s_im + in_re, ars[q] * s_im + ais[q] * s_re + in_im]
            return tuple(out)

        def ctx_step(k, carry):
            return advance(carry, k, n_ctx - 1 - k)

        def lat_step(k, carry):
            kb = n_lat - 1 - k
            for q in range(S5_PASS):
                fw_re[q, rows(k), :] = carry[2 * q]
                fw_im[q, rows(k), :] = carry[2 * q + 1]
                bw_re[q, rows(kb), :] = carry[2 * q]
                bw_im[q, rows(kb), :] = carry[2 * q + 1]
            return advance(carry, n_ctx + k, n_ctx + kb)

        carry = lax.fori_loop(0, n_ctx, ctx_step, (jnp.zeros((bsz, half), F32),) * (2 * S5_PASS))
        lax.fori_loop(0, n_lat, lat_step, carry)

        for q in range(S5_PASS):
            g8 = first + q
            ent = jnp.concatenate([jnp.where(sel, fw_re[q], bw_re[q]),
                                   jnp.where(sel, fw_im[q], bw_im[q])], axis=1).astype(BF16)
            y = (jnp.dot(load_u(ua_ref, ub_ref, g8, r_lat), m_ref[g8], preferred_element_type=F32)
                 + jnp.dot(ent, wc_ref[g8], preferred_element_type=F32))
            y = _gelu_tanh(y)
            y_a[pl.ds(g8, r_lat, stride=S5_GB), :] = y[:, :128]
            y_b[pl.ds(g8, r_lat, stride=S5_GB), :] = y[:, 128:]
        return 0

    lax.fori_loop(0, S5_GB // S5_PASS, one_pass, 0)

    cpb = math.gcd(n_lat, 16)

    def relayout(i, _):
        src = pl.ds(pl.multiple_of(i * (cpb * bsz * S5_GB), cpb * bsz * S5_GB), cpb * bsz * S5_GB)
        dst = pl.ds(pl.multiple_of(i * (cpb * S5_CHUNK), cpb * S5_CHUNK), cpb * S5_CHUNK)
        t_a = _gran_transpose(y_a[src, :].reshape(cpb * bsz, 8, 128)).reshape(cpb, bsz, 8, 128)
        t_b = _gran_transpose(y_b[src, :].reshape(cpb * bsz, 8, 128)).reshape(cpb, bsz, 8, 128)
        for b in range(bsz):
            tok = jnp.concatenate([t_a[:, b][:, None], t_b[:, b][:, None]], axis=1)
            o_ref[b, dst, :] = tok.reshape(cpb * S5_CHUNK, 128).astype(o_ref.dtype)
        return 0

    lax.fori_loop(0, n_lat // cpb, relayout, 0)


def _s5(ua, ub, uca, ucb, wb, wc, m, ar, ai, n_ctx, n_lat, bsz):
    ngb = ua.shape[0]
    half = 2 * S5_STATE
    w = S5_CHUNK * S5_GROUP
    r_ctx, r_lat = n_ctx * bsz, n_lat * bsz
    u_blk = lambda r: pl.BlockSpec((1, r * S5_GB, 128), lambda i: (i, 0, 0), pipeline_mode=pl.Buffered(1))
    w_blk = lambda shape: pl.BlockSpec((S5_GB,) + shape, lambda i: (i, 0, 0))
    pass_scr = lambda r: pltpu.VMEM((S5_PASS, r, half), F32)
    return pl.pallas_call(
        functools.partial(_s5_kernel, n_ctx=n_ctx, n_lat=n_lat, bsz=bsz),
        grid=(ngb,),
        in_specs=[u_blk(r_lat), u_blk(r_lat), u_blk(r_ctx), u_blk(r_ctx),
                  w_blk((w, 2 * half)), w_blk((2 * half, w)), w_blk((w, w)), w_blk((1, half)), w_blk((1, half))],
        out_specs=pl.BlockSpec((bsz, n_lat * S5_CHUNK, 128), lambda i: (0, 0, i)),
        out_shape=jax.ShapeDtypeStruct((bsz, n_lat * S5_CHUNK, ngb * 128), BF16),
        scratch_shapes=[pass_scr(r_ctx + r_lat)] * 2 + [pass_scr(r_lat)] * 4
                       + [pltpu.VMEM((r_lat * S5_GB, 128), F32)] * 2,
        compiler_params=_params(("parallel",)),
        name="s5",
    )(ua, ub, uca, ucb, wb, wc, m, ar, ai)


def _filt_kernel(z_ref, w1_ref, b1_ref, w2_ref, b2_ref, w3_ref, fr_ref, dec_ref, o_ref, h_scr):
    @pl.when(pl.program_id(0) == 0)
    def _():
        fr = fr_ref[...]
        h = jnp.sin(fr[0:1] * (jnp.dot(z_ref[...], w1_ref[...], precision=HIGHEST, preferred_element_type=F32)
                               + b1_ref[...]))
        h_scr[...] = jnp.sin(fr[1:2] * (jnp.dot(h, w2_ref[...], precision=HIGHEST, preferred_element_type=F32)
                                        + b2_ref[...]))

    h3 = jnp.dot(h_scr[...], w3_ref[...], precision=HIGHEST, preferred_element_type=F32)
    o_ref[...] = h3 * jnp.exp(-z_ref[:, 0:1] * jnp.abs(dec_ref[...]))


def _filters(z, w1, b1, w2, b2, w3, freq, decay, tn=512):
    n, e = z.shape
    f = w1.shape[1]
    nn = w3.shape[1]
    full = lambda shape: pl.BlockSpec(shape, lambda j: (0, 0))
    return pl.pallas_call(
        _filt_kernel,
        grid=(nn // tn,),
        in_specs=[full((n, e)), full((e, f)), full((1, f)), full((f, f)), full((1, f)),
                  pl.BlockSpec((f, tn), lambda j: (0, j)), full((2, f)),
                  pl.BlockSpec((1, tn), lambda j: (0, j))],
        out_specs=pl.BlockSpec((n, tn), lambda j: (0, j)),
        out_shape=jax.ShapeDtypeStruct((n, nn), F32),
        scratch_shapes=[pltpu.VMEM((n, f), F32)],
        compiler_params=_params(("arbitrary",)),
        name="filt",
    )(z, w1, b1.reshape(1, f), w2, b2.reshape(1, f), w3, freq, decay.reshape(1, nn))


def _dft_tables(n):
    k = np.arange(n, dtype=np.int64)
    ang = (np.outer(k, k) % (2 * n)).astype(np.float64) * (math.pi / n)
    return jnp.asarray(np.cos(ang), dtype=BF16), jnp.asarray(-np.sin(ang), dtype=BF16)


def _alt(n):
    return jnp.where(lax.broadcasted_iota(jnp.int32, (n, 1), 0) % 2 == 0, 1.0, -1.0).astype(F32)


def _spec_kernel(ff_ref, fb_ref, c_ref, s_ref, hr_ref, hi_ref, hn_ref):
    ff = ff_ref[...]
    fb = fb_ref[...]
    n = ff.shape[0]
    b0 = fb[0:1]
    den = (jnp.sum(jnp.abs(ff), axis=0, keepdims=True) + jnp.sum(jnp.abs(fb), axis=0, keepdims=True)
           - jnp.abs(b0) + EPS)
    inv = 1.0 / den
    ffb = ff.astype(BF16)
    fbb = fb.astype(BF16)
    cmat = c_ref[...]
    smat = s_ref[...]
    re = (jnp.dot(cmat, ffb, preferred_element_type=F32) + jnp.dot(cmat, fbb, preferred_element_type=F32) - b0)
    im = (jnp.dot(smat, ffb, preferred_element_type=F32) - jnp.dot(smat, fbb, preferred_element_type=F32))
    wk = jnp.where(lax.broadcasted_iota(jnp.int32, (n, 1), 0) == 0, 0.5 / n, 1.0 / n)
    hr_ref[0] = re * (wk * inv)
    hi_ref[0] = im * (wk * inv)
    alt = _alt(n)
    hn_ref[0] = (jnp.sum(ff * alt, axis=0, keepdims=True) + jnp.sum(fb * alt, axis=0, keepdims=True) - b0) * (inv * (0.5 / n))


def _spectrum(filt, cmat, smat, hy_w, tn=256):
    n = filt.shape[0]
    nt = hy_w // tn
    const = lambda shape: pl.BlockSpec(shape, lambda o, j: (0, 0), pipeline_mode=pl.Buffered(1))
    out_blk = lambda r: pl.BlockSpec((1, r, tn), lambda o, j: (o, 0, j))
    return pl.pallas_call(
        _spec_kernel,
        grid=(HY_ORDER, nt),
        in_specs=[pl.BlockSpec((n, tn), lambda o, j: (0, o * 2 * nt + j)),
                  pl.BlockSpec((n, tn), lambda o, j: (0, o * 2 * nt + nt + j)),
                  const((n, n)), const((n, n))],
        out_specs=[out_blk(n), out_blk(n), out_blk(1)],
        out_shape=[jax.ShapeDtypeStruct((HY_ORDER, n, hy_w), F32),
                   jax.ShapeDtypeStruct((HY_ORDER, n, hy_w), F32),
                   jax.ShapeDtypeStruct((HY_ORDER, 1, hy_w), F32)],
        compiler_params=_params(("parallel", "parallel")),
        name="spec",
    )(filt, filt, cmat, smat)


def _hconv_kernel(z_ref, gate_ref, bias_ref, hr_ref, hi_ref, hn_ref, c_ref, s_ref, o_ref, y_scr, *, fb):
    z = z_ref[0]
    n = z.shape[0]
    zf = z.astype(F32)
    alt = _alt(n)
    z_nyq = jnp.sum(zf * alt, axis=0, keepdims=True)
    y_scr[...] = alt * (z_nyq * hn_ref[0]) + bias_ref[0] * zf

    def body(i, carry):
        r0 = pl.multiple_of(i * fb, fb)
        zr = jnp.dot(c_ref[pl.ds(r0, fb), :], z, preferred_element_type=F32)
        zi = jnp.dot(s_ref[pl.ds(r0, fb), :], z, preferred_element_type=F32)
        hr = hr_ref[0, pl.ds(r0, fb), :]
        hi = hi_ref[0, pl.ds(r0, fb), :]
        yr = (zr * hr - zi * hi).astype(BF16)
        yi = (zr * hi + zi * hr).astype(BF16)
        y_scr[...] += (jnp.dot(c_ref[:, pl.ds(r0, fb)], yr, preferred_element_type=F32)
                       + jnp.dot(s_ref[:, pl.ds(r0, fb)], yi, preferred_element_type=F32))
        return carry

    lax.fori_loop(0, n // fb, body, 0)
    o_ref[0] = (gate_ref[0].astype(F32) * y_scr[...]).astype(o_ref.dtype)


def _hconv(zsrc, z_off, gsrc, g_off, bias, hr, hi, hn, order, cmat, smat, hy_w, tc=256, fb=256):
    bsz, n, _ = zsrc.shape
    nt = hy_w // tc
    zo, go = z_off // tc, g_off // tc
    const = lambda shape: pl.BlockSpec(shape, lambda j, b: (0, 0), pipeline_mode=pl.Buffered(1))
    return pl.pallas_call(
        functools.partial(_hconv_kernel, fb=fb),
        grid=(nt, bsz),
        in_specs=[pl.BlockSpec((1, n, tc), lambda j, b: (b, 0, zo + j)),
                  pl.BlockSpec((1, n, tc), lambda j, b: (b, 0, go + j)),
                  pl.BlockSpec((1, 1, tc), lambda j, b: (order, 0, j)),
                  pl.BlockSpec((1, n, tc), lambda j, b: (order, 0, j)),
                  pl.BlockSpec((1, n, tc), lambda j, b: (order, 0, j)),
                  pl.BlockSpec((1, 1, tc), lambda j, b: (order, 0, j)),
                  const((n, n)), const((n, n))],
        out_specs=pl.BlockSpec((1, n, tc), lambda j, b: (b, 0, j)),
        out_shape=jax.ShapeDtypeStruct((bsz, n, hy_w), BF16),
        scratch_shapes=[pltpu.VMEM((n, tc), F32)],
        compiler_params=_params(("parallel", "parallel")),
        name="hconv",
    )(zsrc, gsrc, bias, hr, hi, hn, cmat, smat)


def _mix_kernel(ys_ref, yh_ref, x_ref, gw_ref, gb_ref, gs_ref, gh_ref, wo_ref, g1_ref, o_ref, *, s5_w):
    gl = jnp.dot(ys_ref[0], gw_ref[...], preferred_element_type=F32) + gb_ref[...]
    a = gl[:, :s5_w] * jax.nn.sigmoid(gl[:, s5_w:])
    a = _rms(a, gs_ref[...]).astype(BF16)
    yh = _rms(yh_ref[0].astype(F32), gh_ref[...]).astype(BF16)
    proj = (jnp.dot(a, wo_ref[:s5_w, :], preferred_element_type=F32)
            + jnp.dot(yh, wo_ref[s5_w:, :], preferred_element_type=F32))
    o_ref[0] = x_ref[0] + g1_ref[0] * proj


def _mix(ys, yh, x, glu_w, glu_b, g_s5, g_hy, w_out, g1, tm=512):
    bsz, n, d = x.shape
    tm = min(tm, n)
    s5_w = ys.shape[-1]
    hy_w = yh.shape[-1]
    const = lambda shape: pl.BlockSpec(shape, lambda b, i: (0, 0), pipeline_mode=pl.Buffered(1))
    tok = lambda w: pl.BlockSpec((1, tm, w), lambda b, i: (b, i, 0))
    return pl.pallas_call(
        functools.partial(_mix_kernel, s5_w=s5_w),
        grid=(bsz, n // tm),
        in_specs=[tok(s5_w), tok(hy_w), tok(d),
                  const((s5_w, 2 * s5_w)), const((1, 2 * s5_w)), const((1, s5_w)), const((1, hy_w)),
                  const((s5_w + hy_w, d)),
                  pl.BlockSpec((1, 1, d), lambda b, i: (b, 0, 0))],
        out_specs=tok(d),
        out_shape=jax.ShapeDtypeStruct((bsz, n, d), F32),
        compiler_params=_params(("parallel", "parallel")),
        name="mix",
    )(ys, yh, x, glu_w, glu_b.reshape(1, -1), g_s5.reshape(1, -1), g_hy.reshape(1, -1), w_out, g1)


def _ffn_kernel(x_ref, g_ref, sh_ref, sc_ref, wg_ref, wu_ref, wd_ref, g2_ref, fg_ref, o_ref, h_scr, acc_scr):
    j = pl.program_id(2)

    @pl.when(j == 0)
    def _():
        h = _rms(x_ref[0], g_ref[...]) * (1.0 + sc_ref[0]) + sh_ref[0]
        h_scr[...] = h.astype(BF16)
        acc_scr[...] = jnp.zeros_like(acc_scr)

    h = h_scr[...]
    gate = jnp.dot(h, wg_ref[...], preferred_element_type=F32)
    up = jnp.dot(h, wu_ref[...], preferred_element_type=F32)
    act = (gate * jax.nn.sigmoid(gate) * up).astype(BF16)
    acc_scr[...] += jnp.dot(act, wd_ref[...], preferred_element_type=F32)

    @pl.when(j == pl.num_programs(2) - 1)
    def _():
        o_ref[0] = _rms(x_ref[0] + g2_ref[0] * acc_scr[...], fg_ref[...])


def _ffn(x, g, sh, sc, wg, wu, wd, g2, fg, tm=512, tf=512):
    bsz, n, d = x.shape
    tm = min(tm, n)
    dff = wg.shape[1]
    row = lambda: pl.BlockSpec((1, d), lambda b, i, j: (0, 0))
    per_b = lambda: pl.BlockSpec((1, 1, d), lambda b, i, j: (b, 0, 0))
    return pl.pallas_call(
        _ffn_kernel,
        grid=(bsz, n // tm, dff // tf),
        in_specs=[pl.BlockSpec((1, tm, d), lambda b, i, j: (b, i, 0)),
                  row(), per_b(), per_b(),
                  pl.BlockSpec((d, tf), lambda b, i, j: (0, j)),
                  pl.BlockSpec((d, tf), lambda b, i, j: (0, j)),
                  pl.BlockSpec((tf, d), lambda b, i, j: (j, 0)),
                  per_b(), row()],
        out_specs=pl.BlockSpec((1, tm, d), lambda b, i, j: (b, i, 0)),
        out_shape=jax.ShapeDtypeStruct((bsz, n, d), F32),
        scratch_shapes=[pltpu.VMEM((tm, d), BF16), pltpu.VMEM((tm, d), F32)],
        compiler_params=_params(("parallel", "parallel", "arbitrary")),
        name="ffn",
    )(x, g.reshape(1, d), sh, sc, wg, wu, wd, g2, fg.reshape(1, d))


def _positional_features(n):
    pos = jnp.arange(n, dtype=F32)
    t = pos[:, None] / n
    bands = jnp.linspace(1e-4, HY_BANDS - 1, HY_BANDS, dtype=F32)
    ang = 2.0 * math.pi * pos[:, None] * bands[None, :] / n
    return jnp.concatenate([t, jnp.cos(ang), -jnp.sin(ang)], axis=-1)


def kernel(x, c, ctx, c_ctx, ada_w, ada_b, norm1_g, w_in, conv_w, conv_b, hy_w1, hy_b1, hy_w2, hy_b2, hy_w3,
           hy_sin_freq, hy_decay, hy_bias, s5_lam_re, s5_lam_im, s5_log_dt, s5_b_re, s5_b_im, s5_c_re, s5_c_im,
           s5_d, s5_glu_w, s5_glu_b, branch_g_s5, branch_g_hy, w_out, norm2_g, ffn_w_gate, ffn_w_up,
           ffn_w_down, final_g):
    bsz, n_lat, d = x.shape
    n_ctx = ctx.shape[1]
    assert ada_w.shape[0] == 1, "single-layer block"
    l = 0
    s5_w = s5_glu_w.shape[1]
    hy_w = w_in.shape[2] - s5_w
    hy_w //= 3
    groups = s5_w // S5_GROUP
    assert n_lat % S5_CHUNK == 0 and n_ctx % S5_CHUNK == 0 and n_lat % GRID_W == 0

    pad_rows = -(bsz + 1) % 8
    cc = jnp.concatenate([c, c_ctx[None, :], jnp.zeros((pad_rows, d), F32)], axis=0)
    mod = _ada(cc, ada_w[l], ada_b[l])
    sh1, sc1, g1, sh2, sc2, g2 = [mod[:bsz, None, i * d:(i + 1) * d] for i in range(6)]
    csh1, csc1 = [jnp.broadcast_to(mod[bsz, None, None, i * d:(i + 1) * d], (bsz, 1, d)) for i in range(2)]

    w_in_b = w_in[l].astype(BF16)
    ua, ub, p = _inproj(x, norm1_g[l], sh1, sc1, w_in_b, conv_w[l], conv_b[l], s5_w, tm=min(512, n_lat))
    uca, ucb = _inproj(ctx, norm1_g[l], csh1, csc1, w_in_b[:, :s5_w], None, None, s5_w, tm=min(256, n_ctx))

    wb, wc, m, ar, ai = _s5_prep(s5_lam_re[l], s5_lam_im[l], s5_log_dt[l], s5_b_re[l], s5_b_im[l],
                                 s5_c_re[l], s5_c_im[l], s5_d[l])
    flat = lambda u: u.reshape(u.shape[0], -1, 128)
    ys = _s5(flat(ua), flat(ub), flat(uca), flat(ucb), wb, wc, m, ar, ai,
             n_ctx // S5_CHUNK, n_lat // S5_CHUNK, bsz)

    z = _positional_features(n_lat)
    e_pad = -z.shape[1] % 128
    z = jnp.pad(z, ((0, 0), (0, e_pad)))
    w1 = jnp.pad(hy_w1[l], ((0, e_pad), (0, 0)))
    filt = _filters(z, w1, hy_b1[l], hy_w2[l], hy_b2[l], hy_w3[l], hy_sin_freq[l],
                    hy_decay[l].reshape(-1))
    cmat, smat = _dft_tables(n_lat)
    hr, hi, hn = _spectrum(filt, cmat, smat, hy_w)
    bias = hy_bias[l].reshape(HY_ORDER, 1, hy_w)
    z1 = _hconv(p, 0, p, hy_w, bias, hr, hi, hn, 0, cmat, smat, hy_w)
    yh = _hconv(z1, 0, p, 2 * hy_w, bias, hr, hi, hn, 1, cmat, smat, hy_w)

    x1 = _mix(ys, yh, x, s5_glu_w[l].astype(BF16), s5_glu_b[l], branch_g_s5[l], branch_g_hy[l],
              w_out[l].astype(BF16), g1)

    return _ffn(x1, norm2_g[l], sh2, sc2, ffn_w_gate[l].astype(BF16), ffn_w_up[l].astype(BF16),
                ffn_w_down[l].astype(BF16), g2, final_g)
```

```python
import functools
import math

import numpy as np
import jax
import jax.numpy as jnp
from jax import lax
from jax.experimental import pallas as pl
from jax.experimental.pallas import tpu as pltpu

EPS = 1e-6
GRID_W = 64
S5_GROUP = 16
S5_STATE = 64
S5_CHUNK = 16
HY_ORDER = 2
HY_BANDS = 16
V7X_VMEM_BYTES = 64 * 1024 * 1024
VMEM_LIMIT = 56 * 1024 * 1024

F32 = jnp.float32
BF16 = jnp.bfloat16
HIGHEST = lax.Precision.HIGHEST


def _params(sem):
    return pltpu.CompilerParams(dimension_semantics=sem, vmem_limit_bytes=VMEM_LIMIT)


def _rms(x, g):
    return x * lax.rsqrt(jnp.mean(x * x, axis=-1, keepdims=True) + EPS) * g


def _ada_kernel(c_ref, w_ref, b_ref, o_ref):
    cv = c_ref[...]
    s = cv * jax.nn.sigmoid(cv)
    o_ref[...] = jnp.dot(s, w_ref[...], precision=HIGHEST, preferred_element_type=F32) + b_ref[...]


def _ada(cc, w, b, tn=1024):
    r, d = cc.shape
    n = w.shape[1]
    return pl.pallas_call(
        _ada_kernel,
        grid=(n // tn,),
        in_specs=[pl.BlockSpec((r, d), lambda j: (0, 0)),
                  pl.BlockSpec((d, tn), lambda j: (0, j)),
                  pl.BlockSpec((1, tn), lambda j: (0, j))],
        out_specs=pl.BlockSpec((r, tn), lambda j: (0, j)),
        out_shape=jax.ShapeDtypeStruct((r, n), F32),
        compiler_params=_params(("arbitrary",)),
        name="ada",
    )(cc, w, b.reshape(1, n))


def _gran_transpose(a):
    r = lax.broadcasted_iota(jnp.int32, a.shape, 1)
    q = lax.broadcasted_iota(jnp.int32, a.shape, 2) // S5_GROUP
    for dist in (4, 2, 1):
        rb = (r & dist) != 0
        qb = (q & dist) != 0
        up = pltpu.roll(pltpu.roll(a, 8 - dist, axis=1), S5_GROUP * dist, axis=2)
        dn = pltpu.roll(pltpu.roll(a, dist, axis=1), 128 - S5_GROUP * dist, axis=2)
        a = jnp.where(rb == qb, a, jnp.where(rb, dn, up))
    return a


def _inproj_kernel(*refs, s5_cols, tn, conv):
    if conv:
        x_ref, g_ref, sh_ref, sc_ref, w_ref, cw_ref, cb_ref, ua_ref, ub_ref, o_ref, h_scr = refs
    else:
        x_ref, g_ref, sh_ref, sc_ref, w_ref, ua_ref, ub_ref, h_scr = refs
    tm = x_ref.shape[1]
    h = _rms(x_ref[0], g_ref[...]) * (1.0 + sc_ref[0]) + sh_ref[0]
    h_scr[...] = h.astype(BF16)

    for jn in range(s5_cols // tn):
        acc = jnp.dot(h_scr[...], w_ref[:, jn * tn:(jn + 1) * tn], preferred_element_type=F32)
        for cb in range(tn // 128):
            tile = acc[:, cb * 128:(cb + 1) * 128].reshape(tm // S5_CHUNK, 2, 8, 128)
            gb = jn * (tn // 128) + cb
            ua_ref[gb, :, 0] = _gran_transpose(tile[:, 0])
            ub_ref[gb, :, 0] = _gran_transpose(tile[:, 1])

    if conv:
        col = lax.broadcasted_iota(jnp.int32, (tm, tn), 0) % GRID_W
        for jn in range((w_ref.shape[1] - s5_cols) // tn):
            c0 = jn * tn
            acc = jnp.dot(h_scr[...], w_ref[:, s5_cols + c0:s5_cols + c0 + tn], preferred_element_type=F32)
            prev = jnp.where(col == 0, 0.0, pltpu.roll(acc, 1, axis=0))
            nxt = jnp.where(col == GRID_W - 1, 0.0, pltpu.roll(acc, tm - 1, axis=0))
            cw = cw_ref[:, c0:c0 + tn]
            o_ref[0, :, c0:c0 + tn] = (prev * cw[0:1] + acc * cw[1:2] + nxt * cw[2:3]
                                       + cb_ref[:, c0:c0 + tn]).astype(o_ref.dtype)


def _inproj(x, g, sh, sc, w, cw, cb, s5_cols, tm, tn=512):
    bsz, n, d = x.shape
    nn = w.shape[1]
    conv = cw is not None
    assert tm % GRID_W == 0 and n % tm == 0 and s5_cols % tn == 0 and (nn - s5_cols) % tn == 0
    ngb = s5_cols // 128
    const = lambda shape: pl.BlockSpec(shape, lambda b, i: (0,) * len(shape), pipeline_mode=pl.Buffered(1))
    per_b = pl.BlockSpec((1, 1, d), lambda b, i: (b, 0, 0))
    in_specs = [pl.BlockSpec((1, tm, d), lambda b, i: (b, i, 0)), const((1, d)), per_b, per_b, const((d, nn))]
    args = [x, g.reshape(1, d), sh, sc, w]
    u_spec = pl.BlockSpec((ngb, tm // S5_CHUNK, 1, 8, 128), lambda b, i: (0, i, b, 0, 0))
    u_shape = jax.ShapeDtypeStruct((ngb, n // S5_CHUNK, bsz, 8, 128), F32)
    out_specs, out_shape = [u_spec, u_spec], [u_shape, u_shape]
    if conv:
        in_specs += [const((3, nn - s5_cols)), const((1, nn - s5_cols))]
        args += [cw, cb.reshape(1, -1)]
        out_specs.append(pl.BlockSpec((1, tm, nn - s5_cols), lambda b, i: (b, i, 0)))
        out_shape.append(jax.ShapeDtypeStruct((bsz, n, nn - s5_cols), BF16))
    return pl.pallas_call(
        functools.partial(_inproj_kernel, s5_cols=s5_cols, tn=tn, conv=conv),
        grid=(bsz, n // tm),
        in_specs=in_specs,
        out_specs=out_specs,
        out_shape=out_shape,
        scratch_shapes=[pltpu.VMEM((tm, d), BF16)],
        compiler_params=_params(("parallel", "parallel")),
        name="inproj" if conv else "inproj_ctx",
    )(*args)


def _s5_prep(lam_re, lam_im, log_dt, b_re, b_im, c_re, c_im, d):
    t = S5_CHUNK
    lam = lax.complex(lam_re.astype(F32), lam_im.astype(F32))
    lam_dt = lam * jnp.exp(log_dt.astype(F32))[..., None]
    a_bar = jnp.exp(lam_dt)
    b_bar = ((a_bar - 1.0) / lam)[..., None] * lax.complex(b_re.astype(F32), b_im.astype(F32))
    cc = lax.complex(c_re.astype(F32), c_im.astype(F32))
    k = jnp.arange(t + 1, dtype=F32)
    apow = jnp.exp(lam_dt[None] * k[:, None, None, None])
    g = lam.shape[1]

    wbf = apow[t - 1::-1, 0].transpose(1, 0, 2)[:, :, None, :] * b_bar[0].transpose(0, 2, 1)[:, None]
    wbb = apow[:t, 1].transpose(1, 0, 2)[:, :, None, :] * b_bar[1].transpose(0, 2, 1)[:, None]
    wb = jnp.concatenate([wbf.real, wbb.real, wbf.imag, wbb.imag], axis=-1).reshape(g, t * S5_GROUP, 4 * S5_STATE)

    wcf = cc[0].transpose(0, 2, 1)[:, :, None, :] * apow[1:, 0].transpose(1, 2, 0)[..., None]
    wcb = cc[1].transpose(0, 2, 1)[:, :, None, :] * apow[t:0:-1, 1].transpose(1, 2, 0)[..., None]
    wc = jnp.concatenate([wcf.real, wcb.real, -wcf.imag, -wcb.imag], axis=1).reshape(g, 4 * S5_STATE, t * S5_GROUP)

    def lag_kernel(dr):
        kk = jnp.einsum('gcp,kgp,gpe->kgec', cc[dr], apow[:t, dr], b_bar[dr], precision=HIGHEST)
        return kk.real

    kf, kb = lag_kernel(0), lag_kernel(1)
    kf = kf.at[0].add(jnp.eye(S5_GROUP, dtype=F32) * d.astype(F32)[:, None, :])
    lag = np.arange(t)[None, :] - np.arange(t)[:, None]
    place = np.concatenate([(lag[None] == np.arange(t)[:, None, None]), (-lag[None] == np.arange(t)[:, None, None])])
    m = jnp.einsum('kst,kgec->gsetc', jnp.asarray(place, F32), jnp.concatenate([kf, kb]), precision=HIGHEST)
    m = m.reshape(g, t * S5_GROUP, t * S5_GROUP)

    a16 = apow[t]
    ar = jnp.concatenate([a16[0].real, a16[1].real], axis=-1)[:, None, :]
    ai = jnp.concatenate([a16[0].imag, a16[1].imag], axis=-1)[:, None, :]
    return wb.astype(BF16), wc.astype(BF16), m.astype(BF16), ar, ai


def _gelu_tanh(x):
    return 0.5 * x * (1.0 + jnp.tanh(math.sqrt(2.0 / math.pi) * (x + 0.044715 * (x * x * x))))


S5_GB = 8
S5_PASS = 4


def _s5_kernel(ua_ref, ub_ref, uca_ref, ucb_ref, wb_ref, wc_ref, m_ref, ar_ref, ai_ref, o_ref,
               p_re, p_im, fw_re, fw_im, bw_re, bw_im, y_a, y_b, *, n_ctx, n_lat, bsz):
    half = 2 * S5_STATE
    r_ctx, r_lat = n_ctx * bsz, n_lat * bsz
    is_fwd = lax.broadcasted_iota(jnp.int32, (bsz, half), 1) < S5_STATE
    sel = lax.broadcasted_iota(jnp.int32, (r_lat, half), 1) < S5_STATE

    def load_u(a_ref, b_ref, g8, r):
        return jnp.concatenate([a_ref[0, pl.ds(g8, r, stride=S5_GB), :],
                                b_ref[0, pl.ds(g8, r, stride=S5_GB), :]], axis=1).astype(BF16)

    def rows(i):
        return pl.ds(pl.multiple_of(i * bsz, bsz), bsz)

    def one_pass(pass_idx, _):
        first = pass_idx * S5_PASS
        for q in range(S5_PASS):
            wbg = wb_ref[first + q]
            pc = jnp.dot(load_u(uca_ref, ucb_ref, first + q, r_ctx), wbg, preferred_element_type=F32)
            pn = jnp.dot(load_u(ua_ref, ub_ref, first + q, r_lat), wbg, preferred_element_type=F32)
            p_re[q, 0:r_ctx, :] = pc[:, :half]
            p_im[q, 0:r_ctx, :] = pc[:, half:]
            p_re[q, r_ctx:, :] = pn[:, :half]
            p_im[q, r_ctx:, :] = pn[:, half:]

        ars = [jnp.broadcast_to(ar_ref[first + q], (bsz, half)) for q in range(S5_PASS)]
        ais = [jnp.broadcast_to(ai_ref[first + q], (bsz, half)) for q in range(S5_PASS)]

        def advance(carry, fi, bi):
            out = []
            for q in range(S5_PASS):
                s_re, s_im = carry[2 * q], carry[2 * q + 1]
                in_re = jnp.where(is_fwd, p_re[q, rows(fi), :], p_re[q, rows(bi), :])
                in_im = jnp.where(is_fwd, p_im[q, rows(fi), :], p_im[q, rows(bi), :])
                out += [ars[q] * s_re - ais[q] * s_im + in_re, ars[q] * s_im + ais[q] * s_re + in_im]
            return tuple(out)

        def ctx_step(k, carry):
            return advance(carry, k, n_ctx - 1 - k)

        def lat_step(k, carry):
            kb = n_lat - 1 - k
            for q in range(S5_PASS):
                fw_re[q, rows(k), :] = carry[2 * q]
                fw_im[q, rows(k), :] = carry[2 * q + 1]
                bw_re[q, rows(kb), :] = carry[2 * q]
                bw_im[q, rows(kb), :] = carry[2 * q + 1]
            return advance(carry, n_ctx + k, n_ctx + kb)

        carry = lax.fori_loop(0, n_ctx, ctx_step, (jnp.zeros((bsz, half), F32),) * (2 * S5_PASS))
        lax.fori_loop(0, n_lat, lat_step, carry)

        for q in range(S5_PASS):
            g8 = first + q
            ent = jnp.concatenate([jnp.where(sel, fw_re[q], bw_re[q]),
                                   jnp.where(sel, fw_im[q], bw_im[q])], axis=1).astype(BF16)
            y = (jnp.dot(load_u(ua_ref, ub_ref, g8, r_lat), m_ref[g8], preferred_element_type=F32)
                 + jnp.dot(ent, wc_ref[g8], preferred_element_type=F32))
            y = _gelu_tanh(y)
            y_a[pl.ds(g8, r_lat, stride=S5_GB), :] = y[:, :128]
            y_b[pl.ds(g8, r_lat, stride=S5_GB), :] = y[:, 128:]
        return 0

    lax.fori_loop(0, S5_GB // S5_PASS, one_pass, 0)

    cpb = math.gcd(n_lat, 16)

    def relayout(i, _):
        src = pl.ds(pl.multiple_of(i * (cpb * bsz * S5_GB), cpb * bsz * S5_GB), cpb * bsz * S5_GB)
        dst = pl.ds(pl.multiple_of(i * (cpb * S5_CHUNK), cpb * S5_CHUNK), cpb * S5_CHUNK)
        t_a = _gran_transpose(y_a[src, :].reshape(cpb * bsz, 8, 128)).reshape(cpb, bsz, 8, 128)
        t_b = _gran_transpose(y_b[src, :].reshape(cpb * bsz, 8, 128)).reshape(cpb, bsz, 8, 128)
        for b in range(bsz):
            tok = jnp.concatenate([t_a[:, b][:, None], t_b[:, b][:, None]], axis=1)
            o_ref[b, dst, :] = tok.reshape(cpb * S5_CHUNK, 128).astype(o_ref.dtype)
        return 0

    lax.fori_loop(0, n_lat // cpb, relayout, 0)


def _s5(ua, ub, uca, ucb, wb, wc, m, ar, ai, n_ctx, n_lat, bsz):
    ngb = ua.shape[0]
    half = 2 * S5_STATE
    w = S5_CHUNK * S5_GROUP
    r_ctx, r_lat = n_ctx * bsz, n_lat * bsz
    u_blk = lambda r: pl.BlockSpec((1, r * S5_GB, 128), lambda i: (i, 0, 0), pipeline_mode=pl.Buffered(1))
    w_blk = lambda shape: pl.BlockSpec((S5_GB,) + shape, lambda i: (i, 0, 0))
    pass_scr = lambda r: pltpu.VMEM((S5_PASS, r, half), F32)
    return pl.pallas_call(
        functools.partial(_s5_kernel, n_ctx=n_ctx, n_lat=n_lat, bsz=bsz),
        grid=(ngb,),
        in_specs=[u_blk(r_lat), u_blk(r_lat), u_blk(r_ctx), u_blk(r_ctx),
                  w_blk((w, 2 * half)), w_blk((2 * half, w)), w_blk((w, w)), w_blk((1, half)), w_blk((1, half))],
        out_specs=pl.BlockSpec((bsz, n_lat * S5_CHUNK, 128), lambda i: (0, 0, i)),
        out_shape=jax.ShapeDtypeStruct((bsz, n_lat * S5_CHUNK, ngb * 128), BF16),
        scratch_shapes=[pass_scr(r_ctx + r_lat)] * 2 + [pass_scr(r_lat)] * 4
                       + [pltpu.VMEM((r_lat * S5_GB, 128), F32)] * 2,
        compiler_params=_params(("parallel",)),
        name="s5",
    )(ua, ub, uca, ucb, wb, wc, m, ar, ai)


def _filt_kernel(z_ref, w1_ref, b1_ref, w2_ref, b2_ref, w3b_ref, w3f_ref, fr_ref, decb_ref, decf_ref, o_ref, h_scr):
    n = z_ref.shape[0] // 2

    @pl.when((pl.program_id(0) == 0) & (pl.program_id(1) == 0))
    def _():
        fr = fr_ref[...]
        h = jnp.sin(fr[0:1] * (jnp.dot(z_ref[...], w1_ref[...], precision=HIGHEST, preferred_element_type=F32)
                               + b1_ref[...]))
        h_scr[...] = jnp.sin(fr[1:2] * (jnp.dot(h, w2_ref[...], precision=HIGHEST, preferred_element_type=F32)
                                        + b2_ref[...]))

    bwd = (jnp.dot(h_scr[0:n, :], w3b_ref[...], precision=HIGHEST, preferred_element_type=F32)
           * jnp.exp(-z_ref[0:n, 0:1] * jnp.abs(decb_ref[...])))
    bwd = jnp.where(lax.broadcasted_iota(jnp.int32, bwd.shape, 0) == 0, 0.0, bwd)
    fwd = (jnp.dot(h_scr[n:, :], w3f_ref[...], precision=HIGHEST, preferred_element_type=F32)
           * jnp.exp(-z_ref[n:, 0:1] * jnp.abs(decf_ref[...])))
    den = jnp.sum(jnp.abs(bwd), axis=0, keepdims=True) + jnp.sum(jnp.abs(fwd), axis=0, keepdims=True) + EPS
    o_ref[0, 0:n, :] = bwd / den
    o_ref[0, n:, :] = fwd / den


def _filters(z2, w1, b1, w2, b2, w3, freq, decay, hy_w, tn=512):
    n2, e = z2.shape
    f = w1.shape[1]
    nt = hy_w // tn
    full = lambda shape: pl.BlockSpec(shape, lambda o, j: (0, 0))
    fwd_col = lambda r: pl.BlockSpec((r, tn), lambda o, j: (0, o * 2 * nt + j))
    bwd_col = lambda r: pl.BlockSpec((r, tn), lambda o, j: (0, o * 2 * nt + nt + j))
    dec = decay.reshape(1, -1)
    return pl.pallas_call(
        _filt_kernel,
        grid=(HY_ORDER, nt),
        in_specs=[full((n2, e)), full((e, f)), full((1, f)), full((f, f)), full((1, f)),
                  bwd_col(f), fwd_col(f), full((2, f)), bwd_col(1), fwd_col(1)],
        out_specs=pl.BlockSpec((1, n2, tn), lambda o, j: (o, 0, j)),
        out_shape=jax.ShapeDtypeStruct((HY_ORDER, n2, hy_w), F32),
        scratch_shapes=[pltpu.VMEM((n2, f), F32)],
        compiler_params=_params(("arbitrary", "arbitrary")),
        name="filt",
    )(z2, w1, b1.reshape(1, f), w2, b2.reshape(1, f), w3, w3, freq, dec, dec)


HY_BLOCKS = 4


def _dft_tables(n):
    k = np.arange(n, dtype=np.int64)
    ang = (np.outer(k, k) % (2 * n)).astype(np.float64) * (math.pi / n)
    return jnp.asarray(np.cos(ang), dtype=BF16), jnp.asarray(-np.sin(ang), dtype=BF16)


def _alt(n):
    return jnp.where(lax.broadcasted_iota(jnp.int32, (n, 1), 0) % 2 == 0, 1.0, -1.0).astype(F32)


def _spec_kernel(h_ref, c_ref, s_ref, hr_ref, hi_ref, hn_ref):
    m = c_ref.shape[0]
    cmat = c_ref[...]
    smat = s_ref[...]
    alt = _alt(m)
    wk = jnp.where(lax.broadcasted_iota(jnp.int32, (m, 1), 0) == 0, 0.5 / m, 1.0 / m)
    prev = None
    for j in range(2 * HY_BLOCKS):
        blk = h_ref[0, j * m:(j + 1) * m, :]
        bb = blk.astype(BF16)
        cur = (jnp.dot(cmat, bb, preferred_element_type=F32), jnp.dot(smat, bb, preferred_element_type=F32),
               jnp.sum(blk * alt, axis=0, keepdims=True), blk[0:1])
        if prev is not None:
            hr_ref[0, j - 1] = (cur[0] + alt * (prev[0] - prev[3])) * wk
            hi_ref[0, j - 1] = (cur[1] + alt * prev[1]) * wk
            hn_ref[0, j - 1] = (cur[2] + prev[2] - prev[3]) * (0.5 / m)
        prev = cur


def _spectrum(taps, cmat, smat, tn=256):
    _, n2, hy_w = taps.shape
    m = cmat.shape[0]
    nd = 2 * HY_BLOCKS - 1
    const = lambda shape: pl.BlockSpec(shape, lambda o, j: (0, 0), pipeline_mode=pl.Buffered(1))
    out_blk = lambda r: pl.BlockSpec((1, nd, r, tn), lambda o, j: (o, 0, 0, j))
    return pl.pallas_call(
        _spec_kernel,
        grid=(HY_ORDER, hy_w // tn),
        in_specs=[pl.BlockSpec((1, n2, tn), lambda o, j: (o, 0, j)), const((m, m)), const((m, m))],
        out_specs=[out_blk(m), out_blk(m), out_blk(1)],
        out_shape=[jax.ShapeDtypeStruct((HY_ORDER, nd, m, hy_w), F32),
                   jax.ShapeDtypeStruct((HY_ORDER, nd, m, hy_w), F32),
                   jax.ShapeDtypeStruct((HY_ORDER, nd, 1, hy_w), F32)],
        compiler_params=_params(("parallel", "parallel")),
        name="spec",
    )(taps, cmat, smat)


def _hconv_kernel(z_ref, gate_ref, bias_ref, hr_ref, hi_ref, hn_ref, c_ref, s_ref, o_ref):
    m = c_ref.shape[0]
    cmat = c_ref[...]
    smat = s_ref[...]
    alt = _alt(m)
    zr, zi, zn = [], [], []
    for j in range(HY_BLOCKS):
        zj = z_ref[0, j * m:(j + 1) * m, :]
        zr.append(jnp.dot(cmat, zj, preferred_element_type=F32))
        zi.append(jnp.dot(smat, zj, preferred_element_type=F32))
        zn.append(jnp.sum(zj.astype(F32) * alt, axis=0, keepdims=True))
    for i in range(HY_BLOCKS):
        yr = yi = yn = None
        for j in range(HY_BLOCKS):
            d = i - j + HY_BLOCKS - 1
            hr = hr_ref[0, d]
            hi = hi_ref[0, d]
            pr = zr[j] * hr - zi[j] * hi
            pi = zr[j] * hi + zi[j] * hr
            pn = zn[j] * hn_ref[0, d]
            yr, yi, yn = (pr, pi, pn) if yr is None else (yr + pr, yi + pi, yn + pn)
        rows = slice(i * m, (i + 1) * m)
        y = (jnp.dot(cmat, yr.astype(BF16), preferred_element_type=F32)
             + jnp.dot(smat, yi.astype(BF16), preferred_element_type=F32)
             + alt * yn + bias_ref[0] * z_ref[0, rows, :].astype(F32))
        o_ref[0, rows, :] = (gate_ref[0, rows, :].astype(F32) * y).astype(o_ref.dtype)


def _hconv(zsrc, z_off, gsrc, g_off, bias, hr, hi, hn, order, cmat, smat, hy_w, tc=256):
    bsz, n, _ = zsrc.shape
    m = cmat.shape[0]
    nd = 2 * HY_BLOCKS - 1
    nt = hy_w // tc
    zo, go = z_off // tc, g_off // tc
    const = lambda shape: pl.BlockSpec(shape, lambda j, b: (0, 0), pipeline_mode=pl.Buffered(1))
    spec_blk = lambda r: pl.BlockSpec((1, nd, r, tc), lambda j, b: (order, 0, 0, j))
    return pl.pallas_call(
        _hconv_kernel,
        grid=(nt, bsz),
        in_specs=[pl.BlockSpec((1, n, tc), lambda j, b: (b, 0, zo + j)),
                  pl.BlockSpec((1, n, tc), lambda j, b: (b, 0, go + j)),
                  pl.BlockSpec((1, 1, tc), lambda j, b: (order, 0, j)),
                  spec_blk(m), spec_blk(m), spec_blk(1),
                  const((m, m)), const((m, m))],
        out_specs=pl.BlockSpec((1, n, tc), lambda j, b: (b, 0, j)),
        out_shape=jax.ShapeDtypeStruct((bsz, n, hy_w), BF16),
        compiler_params=_params(("parallel", "parallel")),
        name="hconv",
    )(zsrc, gsrc, bias, hr, hi, hn, cmat, smat)


def _mix_kernel(ys_ref, yh_ref, x_ref, gw_ref, gb_ref, gs_ref, gh_ref, wo_ref, g1_ref, o_ref, *, s5_w):
    gl = jnp.dot(ys_ref[0], gw_ref[...], preferred_element_type=F32) + gb_ref[...]
    a = gl[:, :s5_w] * jax.nn.sigmoid(gl[:, s5_w:])
    a = _rms(a, gs_ref[...]).astype(BF16)
    yh = _rms(yh_ref[0].astype(F32), gh_ref[...]).astype(BF16)
    proj = (jnp.dot(a, wo_ref[:s5_w, :], preferred_element_type=F32)
            + jnp.dot(yh, wo_ref[s5_w:, :], preferred_element_type=F32))
    o_ref[0] = x_ref[0] + g1_ref[0] * proj


def _mix(ys, yh, x, glu_w, glu_b, g_s5, g_hy, w_out, g1, tm=512):
    bsz, n, d = x.shape
    tm = min(tm, n)
    s5_w = ys.shape[-1]
    hy_w = yh.shape[-1]
    const = lambda shape: pl.BlockSpec(shape, lambda b, i: (0, 0), pipeline_mode=pl.Buffered(1))
    tok = lambda w: pl.BlockSpec((1, tm, w), lambda b, i: (b, i, 0))
    return pl.pallas_call(
        functools.partial(_mix_kernel, s5_w=s5_w),
        grid=(bsz, n // tm),
        in_specs=[tok(s5_w), tok(hy_w), tok(d),
                  const((s5_w, 2 * s5_w)), const((1, 2 * s5_w)), const((1, s5_w)), const((1, hy_w)),
                  const((s5_w + hy_w, d)),
                  pl.BlockSpec((1, 1, d), lambda b, i: (b, 0, 0))],
        out_specs=tok(d),
        out_shape=jax.ShapeDtypeStruct((bsz, n, d), F32),
        compiler_params=_params(("parallel", "parallel")),
        name="mix",
    )(ys, yh, x, glu_w, glu_b.reshape(1, -1), g_s5.reshape(1, -1), g_hy.reshape(1, -1), w_out, g1)


def _ffn_kernel(x_ref, g_ref, sh_ref, sc_ref, wg_ref, wu_ref, wd_ref, g2_ref, fg_ref, o_ref, h_scr, acc_scr):
    j = pl.program_id(2)

    @pl.when(j == 0)
    def _():
        h = _rms(x_ref[0], g_ref[...]) * (1.0 + sc_ref[0]) + sh_ref[0]
        h_scr[...] = h.astype(BF16)
        acc_scr[...] = jnp.zeros_like(acc_scr)

    h = h_scr[...]
    gate = jnp.dot(h, wg_ref[...], preferred_element_type=F32)
    up = jnp.dot(h, wu_ref[...], preferred_element_type=F32)
    act = (gate * jax.nn.sigmoid(gate) * up).astype(BF16)
    acc_scr[...] += jnp.dot(act, wd_ref[...], preferred_element_type=F32)

    @pl.when(j == pl.num_programs(2) - 1)
    def _():
        o_ref[0] = _rms(x_ref[0] + g2_ref[0] * acc_scr[...], fg_ref[...])


def _ffn(x, g, sh, sc, wg, wu, wd, g2, fg, tm=512, tf=512):
    bsz, n, d = x.shape
    tm = min(tm, n)
    dff = wg.shape[1]
    row = lambda: pl.BlockSpec((1, d), lambda b, i, j: (0, 0))
    per_b = lambda: pl.BlockSpec((1, 1, d), lambda b, i, j: (b, 0, 0))
    return pl.pallas_call(
        _ffn_kernel,
        grid=(bsz, n // tm, dff // tf),
        in_specs=[pl.BlockSpec((1, tm, d), lambda b, i, j: (b, i, 0)),
                  row(), per_b(), per_b(),
                  pl.BlockSpec((d, tf), lambda b, i, j: (0, j)),
                  pl.BlockSpec((d, tf), lambda b, i, j: (0, j)),
                  pl.BlockSpec((tf, d), lambda b, i, j: (j, 0)),
                  per_b(), row()],
        out_specs=pl.BlockSpec((1, tm, d), lambda b, i, j: (b, i, 0)),
        out_shape=jax.ShapeDtypeStruct((bsz, n, d), F32),
        scratch_shapes=[pltpu.VMEM((tm, d), BF16), pltpu.VMEM((tm, d), F32)],
        compiler_params=_params(("parallel", "parallel", "arbitrary")),
        name="ffn",
    )(x, g.reshape(1, d), sh, sc, wg, wu, wd, g2, fg.reshape(1, d))


def _positional_features(n):
    pos = jnp.abs(jnp.arange(2 * n, dtype=F32) - n)
    t = pos[:, None] / n
    bands = jnp.linspace(1e-4, HY_BANDS - 1, HY_BANDS, dtype=F32)
    ang = 2.0 * math.pi * pos[:, None] * bands[None, :] / n
    return jnp.concatenate([t, jnp.cos(ang), -jnp.sin(ang)], axis=-1)


def kernel(x, c, ctx, c_ctx, ada_w, ada_b, norm1_g, w_in, conv_w, conv_b, hy_w1, hy_b1, hy_w2, hy_b2, hy_w3,
           hy_sin_freq, hy_decay, hy_bias, s5_lam_re, s5_lam_im, s5_log_dt, s5_b_re, s5_b_im, s5_c_re, s5_c_im,
           s5_d, s5_glu_w, s5_glu_b, branch_g_s5, branch_g_hy, w_out, norm2_g, ffn_w_gate, ffn_w_up,
           ffn_w_down, final_g):
    bsz, n_lat, d = x.shape
    n_ctx = ctx.shape[1]
    assert ada_w.shape[0] == 1, "single-layer block"
    l = 0
    s5_w = s5_glu_w.shape[1]
    hy_w = w_in.shape[2] - s5_w
    hy_w //= 3
    groups = s5_w // S5_GROUP
    assert n_lat % S5_CHUNK == 0 and n_ctx % S5_CHUNK == 0 and n_lat % GRID_W == 0

    pad_rows = -(bsz + 1) % 8
    cc = jnp.concatenate([c, c_ctx[None, :], jnp.zeros((pad_rows, d), F32)], axis=0)
    mod = _ada(cc, ada_w[l], ada_b[l])
    sh1, sc1, g1, sh2, sc2, g2 = [mod[:bsz, None, i * d:(i + 1) * d] for i in range(6)]
    csh1, csc1 = [jnp.broadcast_to(mod[bsz, None, None, i * d:(i + 1) * d], (bsz, 1, d)) for i in range(2)]

    w_in_b = w_in[l].astype(BF16)
    ua, ub, p = _inproj(x, norm1_g[l], sh1, sc1, w_in_b, conv_w[l], conv_b[l], s5_w, tm=min(512, n_lat))
    uca, ucb = _inproj(ctx, norm1_g[l], csh1, csc1, w_in_b[:, :s5_w], None, None, s5_w, tm=min(256, n_ctx))

    wb, wc, m, ar, ai = _s5_prep(s5_lam_re[l], s5_lam_im[l], s5_log_dt[l], s5_b_re[l], s5_b_im[l],
                                 s5_c_re[l], s5_c_im[l], s5_d[l])
    flat = lambda u: u.reshape(u.shape[0], -1, 128)
    ys = _s5(flat(ua), flat(ub), flat(uca), flat(ucb), wb, wc, m, ar, ai,
             n_ctx // S5_CHUNK, n_lat // S5_CHUNK, bsz)

    z = _positional_features(n_lat)
    e_pad = -z.shape[1] % 128
    z = jnp.pad(z, ((0, 0), (0, e_pad)))
    w1 = jnp.pad(hy_w1[l], ((0, e_pad), (0, 0)))
    taps = _filters(z, w1, hy_b1[l], hy_w2[l], hy_b2[l], hy_w3[l], hy_sin_freq[l],
                    hy_decay[l].reshape(-1), hy_w)
    assert n_lat % (HY_BLOCKS * 128) == 0
    cmat, smat = _dft_tables(n_lat // HY_BLOCKS)
    hr, hi, hn = _spectrum(taps, cmat, smat)
    bias = hy_bias[l].reshape(HY_ORDER, 1, hy_w)
    z1 = _hconv(p, 0, p, hy_w, bias, hr, hi, hn, 0, cmat, smat, hy_w)
    yh = _hconv(z1, 0, p, 2 * hy_w, bias, hr, hi, hn, 1, cmat, smat, hy_w)

    x1 = _mix(ys, yh, x, s5_glu_w[l].astype(BF16), s5_glu_b[l], branch_g_s5[l], branch_g_hy[l],
              w_out[l].astype(BF16), g1)

    return _ffn(x1, norm2_g[l], sh2, sc2, ffn_w_gate[l].astype(BF16), ffn_w_up[l].astype(BF16),
                ffn_w_down[l].astype(BF16), g2, final_g)
```

```python
import functools
import math

import numpy as np
import jax
import jax.numpy as jnp
from jax import lax
from jax.experimental import pallas as pl
from jax.experimental.pallas import tpu as pltpu

EPS = 1e-6
GRID_W = 64
S5_GROUP = 16
S5_STATE = 64
S5_CHUNK = 16
HY_ORDER = 2
HY_BANDS = 16
V7X_VMEM_BYTES = 64 * 1024 * 1024
VMEM_LIMIT = 56 * 1024 * 1024

F32 = jnp.float32
BF16 = jnp.bfloat16
HIGHEST = lax.Precision.HIGHEST


def _params(sem):
    return pltpu.CompilerParams(dimension_semantics=sem, vmem_limit_bytes=VMEM_LIMIT)


def _rms(x, g):
    return x * lax.rsqrt(jnp.mean(x * x, axis=-1, keepdims=True) + EPS) * g


def _ada_kernel(c_ref, w_ref, b_ref, o_ref):
    cv = c_ref[...]
    s = cv * jax.nn.sigmoid(cv)
    o_ref[...] = jnp.dot(s, w_ref[...], precision=HIGHEST, preferred_element_type=F32) + b_ref[...]


def _ada(cc, w, b, tn=1024):
    r, d = cc.shape
    n = w.shape[1]
    return pl.pallas_call(
        _ada_kernel,
        grid=(n // tn,),
        in_specs=[pl.BlockSpec((r, d), lambda j: (0, 0)),
                  pl.BlockSpec((d, tn), lambda j: (0, j)),
                  pl.BlockSpec((1, tn), lambda j: (0, j))],
        out_specs=pl.BlockSpec((r, tn), lambda j: (0, j)),
        out_shape=jax.ShapeDtypeStruct((r, n), F32),
        compiler_params=_params(("arbitrary",)),
        name="ada",
    )(cc, w, b.reshape(1, n))


def _gran_transpose(a):
    r = lax.broadcasted_iota(jnp.int32, a.shape, 1)
    q = lax.broadcasted_iota(jnp.int32, a.shape, 2) // S5_GROUP
    for dist in (4, 2, 1):
        rb = (r & dist) != 0
        qb = (q & dist) != 0
        up = pltpu.roll(pltpu.roll(a, 8 - dist, axis=1), S5_GROUP * dist, axis=2)
        dn = pltpu.roll(pltpu.roll(a, dist, axis=1), 128 - S5_GROUP * dist, axis=2)
        a = jnp.where(rb == qb, a, jnp.where(rb, dn, up))
    return a


def _inproj_kernel(*refs, s5_cols, tn, conv):
    if conv:
        x_ref, g_ref, sh_ref, sc_ref, w_ref, cw_ref, cb_ref, ua_ref, ub_ref, o_ref, h_scr = refs
    else:
        x_ref, g_ref, sh_ref, sc_ref, w_ref, ua_ref, ub_ref, h_scr = refs
    tm = x_ref.shape[1]
    h = _rms(x_ref[0], g_ref[...]) * (1.0 + sc_ref[0]) + sh_ref[0]
    h_scr[...] = h.astype(BF16)

    for jn in range(s5_cols // tn):
        acc = jnp.dot(h_scr[...], w_ref[:, jn * tn:(jn + 1) * tn], preferred_element_type=F32)
        for cb in range(tn // 128):
            tile = acc[:, cb * 128:(cb + 1) * 128].reshape(tm // S5_CHUNK, 2, 8, 128)
            gb = jn * (tn // 128) + cb
            ua_ref[gb, :, 0] = _gran_transpose(tile[:, 0])
            ub_ref[gb, :, 0] = _gran_transpose(tile[:, 1])

    if conv:
        col = lax.broadcasted_iota(jnp.int32, (tm, tn), 0) % GRID_W
        for jn in range((w_ref.shape[1] - s5_cols) // tn):
            c0 = jn * tn
            acc = jnp.dot(h_scr[...], w_ref[:, s5_cols + c0:s5_cols + c0 + tn], preferred_element_type=F32)
            prev = jnp.where(col == 0, 0.0, pltpu.roll(acc, 1, axis=0))
            nxt = jnp.where(col == GRID_W - 1, 0.0, pltpu.roll(acc, tm - 1, axis=0))
            cw = cw_ref[:, c0:c0 + tn]
            o_ref[0, :, c0:c0 + tn] = (prev * cw[0:1] + acc * cw[1:2] + nxt * cw[2:3]
                                       + cb_ref[:, c0:c0 + tn]).astype(o_ref.dtype)


def _inproj(x, g, sh, sc, w, cw, cb, s5_cols, tm, tn=512):
    bsz, n, d = x.shape
    nn = w.shape[1]
    conv = cw is not None
    assert tm % GRID_W == 0 and n % tm == 0 and s5_cols % tn == 0 and (nn - s5_cols) % tn == 0
    ngb = s5_cols // 128
    const = lambda shape: pl.BlockSpec(shape, lambda b, i: (0,) * len(shape), pipeline_mode=pl.Buffered(1))
    per_b = pl.BlockSpec((1, 1, d), lambda b, i: (b, 0, 0))
    in_specs = [pl.BlockSpec((1, tm, d), lambda b, i: (b, i, 0)), const((1, d)), per_b, per_b, const((d, nn))]
    args = [x, g.reshape(1, d), sh, sc, w]
    u_spec = pl.BlockSpec((ngb, tm // S5_CHUNK, 1, 8, 128), lambda b, i: (0, i, b, 0, 0))
    u_shape = jax.ShapeDtypeStruct((ngb, n // S5_CHUNK, bsz, 8, 128), F32)
    out_specs, out_shape = [u_spec, u_spec], [u_shape, u_shape]
    if conv:
        in_specs += [const((3, nn - s5_cols)), const((1, nn - s5_cols))]
        args += [cw, cb.reshape(1, -1)]
        out_specs.append(pl.BlockSpec((1, tm, nn - s5_cols), lambda b, i: (b, i, 0)))
        out_shape.append(jax.ShapeDtypeStruct((bsz, n, nn - s5_cols), BF16))
    return pl.pallas_call(
        functools.partial(_inproj_kernel, s5_cols=s5_cols, tn=tn, conv=conv),
        grid=(bsz, n // tm),
        in_specs=in_specs,
        out_specs=out_specs,
        out_shape=out_shape,
        scratch_shapes=[pltpu.VMEM((tm, d), BF16)],
        compiler_params=_params(("parallel", "parallel")),
        name="inproj" if conv else "inproj_ctx",
    )(*args)


def _s5_prep(lam_re, lam_im, log_dt, b_re, b_im, c_re, c_im, d):
    t = S5_CHUNK
    lam_re, lam_im = lam_re.astype(F32), lam_im.astype(F32)
    dt = jnp.exp(log_dt.astype(F32))[..., None]
    k = jnp.arange(t + 1, dtype=F32)[:, None, None, None]
    mag = jnp.exp(lam_re * dt * k)
    pr, pi = mag * jnp.cos(lam_im * dt * k), mag * jnp.sin(lam_im * dt * k)
    er, ei = pr[1] - 1.0, pi[1]
    den = lam_re * lam_re + lam_im * lam_im
    qr, qi = (er * lam_re + ei * lam_im) / den, (ei * lam_re - er * lam_im) / den
    b_re, b_im = b_re.astype(F32), b_im.astype(F32)
    bbr = qr[..., None] * b_re - qi[..., None] * b_im
    bbi = qr[..., None] * b_im + qi[..., None] * b_re
    cr, ci = c_re.astype(F32), c_im.astype(F32)
    g = lam_re.shape[1]

    def times_b(powr, powi, dr):
        ar_, ai_ = powr.transpose(1, 0, 2)[:, :, None, :], powi.transpose(1, 0, 2)[:, :, None, :]
        br_, bi_ = bbr[dr].transpose(0, 2, 1)[:, None], bbi[dr].transpose(0, 2, 1)[:, None]
        return ar_ * br_ - ai_ * bi_, ar_ * bi_ + ai_ * br_

    wbf_r, wbf_i = times_b(pr[t - 1::-1, 0], pi[t - 1::-1, 0], 0)
    wbb_r, wbb_i = times_b(pr[:t, 1], pi[:t, 1], 1)
    wb = jnp.concatenate([wbf_r, wbb_r, wbf_i, wbb_i], axis=-1).reshape(g, t * S5_GROUP, 4 * S5_STATE)

    def c_times(powr, powi, dr):
        cr_, ci_ = cr[dr].transpose(0, 2, 1)[:, :, None, :], ci[dr].transpose(0, 2, 1)[:, :, None, :]
        ar_, ai_ = powr.transpose(1, 2, 0)[..., None], powi.transpose(1, 2, 0)[..., None]
        return cr_ * ar_ - ci_ * ai_, cr_ * ai_ + ci_ * ar_

    wcf_r, wcf_i = c_times(pr[1:, 0], pi[1:, 0], 0)
    wcb_r, wcb_i = c_times(pr[t:0:-1, 1], pi[t:0:-1, 1], 1)
    wc = jnp.concatenate([wcf_r, wcb_r, -wcf_i, -wcb_i], axis=1).reshape(g, 4 * S5_STATE, t * S5_GROUP)

    def lag_kernel(dr):
        abr = pr[:t, dr, :, :, None] * bbr[dr] - pi[:t, dr, :, :, None] * bbi[dr]
        abi = pr[:t, dr, :, :, None] * bbi[dr] + pi[:t, dr, :, :, None] * bbr[dr]
        return (jnp.einsum('kgpe,gcp->kgec', abr, cr[dr], precision=HIGHEST)
                - jnp.einsum('kgpe,gcp->kgec', abi, ci[dr], precision=HIGHEST))

    kf, kb = lag_kernel(0), lag_kernel(1)
    kf = kf.at[0].add(jnp.eye(S5_GROUP, dtype=F32) * d.astype(F32)[:, None, :])
    strip = jnp.concatenate([kb[:0:-1], (kf[0] + kb[0])[None], kf[1:]], axis=0)
    strip = strip.transpose(1, 2, 0, 3).reshape(g, S5_GROUP, (2 * t - 1) * S5_GROUP)
    m = jnp.pad(strip, ((0, 0), (0, 0), (0, S5_GROUP)))

    ar = jnp.concatenate([pr[t, 0], pr[t, 1]], axis=-1)[:, None, :]
    ai = jnp.concatenate([pi[t, 0], pi[t, 1]], axis=-1)[:, None, :]
    return wb.astype(BF16), wc.astype(BF16), m, ar, ai


def _gelu_tanh(x):
    return 0.5 * x * (1.0 + jnp.tanh(math.sqrt(2.0 / math.pi) * (x + 0.044715 * (x * x * x))))


S5_GB = 8
S5_PASS = 4


def _s5_kernel(ua_ref, ub_ref, uca_ref, ucb_ref, wb_ref, wc_ref, m_ref, ar_ref, ai_ref, o_ref,
               p_re, p_im, fw_re, fw_im, bw_re, bw_im, y_a, y_b, m_scr, u_scr, *, n_ctx, n_lat, bsz):
    half = 2 * S5_STATE
    r_ctx, r_lat = n_ctx * bsz, n_lat * bsz
    is_fwd = lax.broadcasted_iota(jnp.int32, (bsz, half), 1) < S5_STATE
    sel = lax.broadcasted_iota(jnp.int32, (r_lat, half), 1) < S5_STATE

    def load_u(a_ref, b_ref, g8, r):
        return jnp.concatenate([a_ref[0, pl.ds(g8, r, stride=S5_GB), :],
                                b_ref[0, pl.ds(g8, r, stride=S5_GB), :]], axis=1).astype(BF16)

    def rows(i):
        return pl.ds(pl.multiple_of(i * bsz, bsz), bsz)

    def one_pass(pass_idx, _):
        first = pass_idx * S5_PASS
        for q in range(S5_PASS):
            wbg = wb_ref[first + q]
            pc = jnp.dot(load_u(uca_ref, ucb_ref, first + q, r_ctx), wbg, preferred_element_type=F32)
            u_scr[q] = load_u(ua_ref, ub_ref, first + q, r_lat)
            pn = jnp.dot(u_scr[q], wbg, preferred_element_type=F32)
            p_re[q, 0:r_ctx, :] = pc[:, :half]
            p_im[q, 0:r_ctx, :] = pc[:, half:]
            p_re[q, r_ctx:, :] = pn[:, :half]
            p_im[q, r_ctx:, :] = pn[:, half:]

        ars = [jnp.broadcast_to(ar_ref[first + q], (bsz, half)) for q in range(S5_PASS)]
        ais = [jnp.broadcast_to(ai_ref[first + q], (bsz, half)) for q in range(S5_PASS)]

        def advance(carry, fi, bi):
            out = []
            for q in range(S5_PASS):
                s_re, s_im = carry[2 * q], carry[2 * q + 1]
                in_re = jnp.where(is_fwd, p_re[q, rows(fi), :], p_re[q, rows(bi), :])
                in_im = jnp.where(is_fwd, p_im[q, rows(fi), :], p_im[q, rows(bi), :])
                out += [ars[q] * s_re - ais[q] * s_im + in_re, ars[q] * s_im + ais[q] * s_re + in_im]
            return tuple(out)

        def ctx_step(k, carry):
            return advance(carry, k, n_ctx - 1 - k)

        def lat_step(k, carry):
            kb = n_lat - 1 - k
            for q in range(S5_PASS):
                fw_re[q, rows(k), :] = carry[2 * q]
                fw_im[q, rows(k), :] = carry[2 * q + 1]
                bw_re[q, rows(kb), :] = carry[2 * q]
                bw_im[q, rows(kb), :] = carry[2 * q + 1]
            return advance(carry, n_ctx + k, n_ctx + kb)

        carry = lax.fori_loop(0, n_ctx, ctx_step, (jnp.zeros((bsz, half), F32),) * (2 * S5_PASS))
        lax.fori_loop(0, n_lat, lat_step, carry)

        for q in range(S5_PASS):
            g8 = first + q
            ent = jnp.concatenate([jnp.where(sel, fw_re[q], bw_re[q]),
                                   jnp.where(sel, fw_im[q], bw_im[q])], axis=1).astype(BF16)
            strip = m_ref[g8]
            for sig in range(S5_CHUNK):
                lo = (S5_CHUNK - 1 - sig) * S5_GROUP
                m_scr[sig * S5_GROUP:(sig + 1) * S5_GROUP, :] = strip[:, lo:lo + S5_CHUNK * S5_GROUP].astype(BF16)
            y = (jnp.dot(u_scr[q], m_scr[...], preferred_element_type=F32)
                 + jnp.dot(ent, wc_ref[g8], preferred_element_type=F32))
            y = _gelu_tanh(y)
            y_a[pl.ds(g8, r_lat, stride=S5_GB), :] = y[:, :128]
            y_b[pl.ds(g8, r_lat, stride=S5_GB), :] = y[:, 128:]
        return 0

    lax.fori_loop(0, S5_GB // S5_PASS, one_pass, 0)

    cpb = math.gcd(n_lat, 16)

    def relayout(i, _):
        src = pl.ds(pl.multiple_of(i * (cpb * bsz * S5_GB), cpb * bsz * S5_GB), cpb * bsz * S5_GB)
        dst = pl.ds(pl.multiple_of(i * (cpb * S5_CHUNK), cpb * S5_CHUNK), cpb * S5_CHUNK)
        t_a = _gran_transpose(y_a[src, :].reshape(cpb * bsz, 8, 128)).reshape(cpb, bsz, 8, 128)
        t_b = _gran_transpose(y_b[src, :].reshape(cpb * bsz, 8, 128)).reshape(cpb, bsz, 8, 128)
        for b in range(bsz):
            tok = jnp.concatenate([t_a[:, b][:, None], t_b[:, b][:, None]], axis=1)
            o_ref[b, dst, :] = tok.reshape(cpb * S5_CHUNK, 128).astype(o_ref.dtype)
        return 0

    lax.fori_loop(0, n_lat // cpb, relayout, 0)


def _s5(ua, ub, uca, ucb, wb, wc, m, ar, ai, n_ctx, n_lat, bsz):
    ngb = ua.shape[0]
    half = 2 * S5_STATE
    w = S5_CHUNK * S5_GROUP
    r_ctx, r_lat = n_ctx * bsz, n_lat * bsz
    u_blk = lambda r: pl.BlockSpec((1, r * S5_GB, 128), lambda i: (i, 0, 0), pipeline_mode=pl.Buffered(1))
    w_blk = lambda shape: pl.BlockSpec((S5_GB,) + shape, lambda i: (i, 0, 0))
    pass_scr = lambda r: pltpu.VMEM((S5_PASS, r, half), F32)
    return pl.pallas_call(
        functools.partial(_s5_kernel, n_ctx=n_ctx, n_lat=n_lat, bsz=bsz),
        grid=(ngb,),
        in_specs=[u_blk(r_lat), u_blk(r_lat), u_blk(r_ctx), u_blk(r_ctx),
                  w_blk((w, 2 * half)), w_blk((2 * half, w)), w_blk(m.shape[1:]), w_blk((1, half)), w_blk((1, half))],
        out_specs=pl.BlockSpec((bsz, n_lat * S5_CHUNK, 128), lambda i: (0, 0, i)),
        out_shape=jax.ShapeDtypeStruct((bsz, n_lat * S5_CHUNK, ngb * 128), BF16),
        scratch_shapes=[pass_scr(r_ctx + r_lat)] * 2 + [pass_scr(r_lat)] * 4
                       + [pltpu.VMEM((r_lat * S5_GB, 128), F32)] * 2
                       + [pltpu.VMEM((w, w), BF16), pltpu.VMEM((S5_PASS, r_lat, w), BF16)],
        compiler_params=_params(("parallel",)),
        name="s5",
    )(ua, ub, uca, ucb, wb, wc, m, ar, ai)


def _filt_kernel(z_ref, w1_ref, b1_ref, w2_ref, b2_ref, w3b_ref, w3f_ref, fr_ref, decb_ref, decf_ref, o_ref, h_scr):
    n = z_ref.shape[0] // 2

    @pl.when((pl.program_id(0) == 0) & (pl.program_id(1) == 0))
    def _():
        fr = fr_ref[...]
        h = jnp.sin(fr[0:1] * (jnp.dot(z_ref[...], w1_ref[...], precision=HIGHEST, preferred_element_type=F32)
                               + b1_ref[...]))
        h_scr[...] = jnp.sin(fr[1:2] * (jnp.dot(h, w2_ref[...], precision=HIGHEST, preferred_element_type=F32)
                                        + b2_ref[...]))

    bwd = (jnp.dot(h_scr[0:n, :], w3b_ref[...], precision=HIGHEST, preferred_element_type=F32)
           * jnp.exp(-z_ref[0:n, 0:1] * jnp.abs(decb_ref[...])))
    bwd = jnp.where(lax.broadcasted_iota(jnp.int32, bwd.shape, 0) == 0, 0.0, bwd)
    fwd = (jnp.dot(h_scr[n:, :], w3f_ref[...], precision=HIGHEST, preferred_element_type=F32)
           * jnp.exp(-z_ref[n:, 0:1] * jnp.abs(decf_ref[...])))
    den = jnp.sum(jnp.abs(bwd), axis=0, keepdims=True) + jnp.sum(jnp.abs(fwd), axis=0, keepdims=True) + EPS
    o_ref[0, 0:n, :] = bwd / den
    o_ref[0, n:, :] = fwd / den


def _filters(z2, w1, b1, w2, b2, w3, freq, decay, hy_w, tn=512):
    n2, e = z2.shape
    f = w1.shape[1]
    nt = hy_w // tn
    full = lambda shape: pl.BlockSpec(shape, lambda o, j: (0, 0))
    fwd_col = lambda r: pl.BlockSpec((r, tn), lambda o, j: (0, o * 2 * nt + j))
    bwd_col = lambda r: pl.BlockSpec((r, tn), lambda o, j: (0, o * 2 * nt + nt + j))
    dec = decay.reshape(1, -1)
    return pl.pallas_call(
        _filt_kernel,
        grid=(HY_ORDER, nt),
        in_specs=[full((n2, e)), full((e, f)), full((1, f)), full((f, f)), full((1, f)),
                  bwd_col(f), fwd_col(f), full((2, f)), bwd_col(1), fwd_col(1)],
        out_specs=pl.BlockSpec((1, n2, tn), lambda o, j: (o, 0, j)),
        out_shape=jax.ShapeDtypeStruct((HY_ORDER, n2, hy_w), F32),
        scratch_shapes=[pltpu.VMEM((n2, f), F32)],
        compiler_params=_params(("arbitrary", "arbitrary")),
        name="filt",
    )(z2, w1, b1.reshape(1, f), w2, b2.reshape(1, f), w3, w3, freq, dec, dec)


HY_BLOCKS = 4


def _dft_tables(n):
    k = np.arange(n, dtype=np.int64)
    ang = (np.outer(k, k) % (2 * n)).astype(np.float64) * (math.pi / n)
    return jnp.asarray(np.cos(ang), dtype=F32).astype(BF16), jnp.asarray(-np.sin(ang), dtype=F32).astype(BF16)


def _alt(n):
    return jnp.where(lax.broadcasted_iota(jnp.int32, (n, 1), 0) % 2 == 0, 1.0, -1.0).astype(F32)


def _spec_kernel(h_ref, c_ref, s_ref, hr_ref, hi_ref, hn_ref):
    m = c_ref.shape[0]
    cmat = c_ref[...]
    smat = s_ref[...]
    alt = _alt(m)
    wk = jnp.where(lax.broadcasted_iota(jnp.int32, (m, 1), 0) == 0, 0.5 / m, 1.0 / m)
    prev = None
    for j in range(2 * HY_BLOCKS):
        blk = h_ref[0, j * m:(j + 1) * m, :]
        bb = blk.astype(BF16)
        cur = (jnp.dot(cmat, bb, preferred_element_type=F32), jnp.dot(smat, bb, preferred_element_type=F32),
               jnp.sum(blk * alt, axis=0, keepdims=True), blk[0:1])
        if prev is not None:
            hr_ref[0, j - 1] = (cur[0] + alt * (prev[0] - prev[3])) * wk
            hi_ref[0, j - 1] = (cur[1] + alt * prev[1]) * wk
            hn_ref[0, j - 1] = (cur[2] + prev[2] - prev[3]) * (0.5 / m)
        prev = cur


def _spectrum(taps, cmat, smat, tn=256):
    _, n2, hy_w = taps.shape
    m = cmat.shape[0]
    nd = 2 * HY_BLOCKS - 1
    const = lambda shape: pl.BlockSpec(shape, lambda o, j: (0, 0), pipeline_mode=pl.Buffered(1))
    out_blk = lambda r: pl.BlockSpec((1, nd, r, tn), lambda o, j: (o, 0, 0, j))
    return pl.pallas_call(
        _spec_kernel,
        grid=(HY_ORDER, hy_w // tn),
        in_specs=[pl.BlockSpec((1, n2, tn), lambda o, j: (o, 0, j)), const((m, m)), const((m, m))],
        out_specs=[out_blk(m), out_blk(m), out_blk(1)],
        out_shape=[jax.ShapeDtypeStruct((HY_ORDER, nd, m, hy_w), F32),
                   jax.ShapeDtypeStruct((HY_ORDER, nd, m, hy_w), F32),
                   jax.ShapeDtypeStruct((HY_ORDER, nd, 1, hy_w), F32)],
        compiler_params=_params(("parallel", "parallel")),
        name="spec",
    )(taps, cmat, smat)


def _hconv_kernel(z_ref, gate_ref, bias_ref, hr_ref, hi_ref, hn_ref, c_ref, s_ref, o_ref):
    m = c_ref.shape[0]
    cmat = c_ref[...]
    smat = s_ref[...]
    alt = _alt(m)
    zr, zi, zn = [], [], []
    for j in range(HY_BLOCKS):
        zj = z_ref[0, j * m:(j + 1) * m, :]
        zr.append(jnp.dot(cmat, zj, preferred_element_type=F32))
        zi.append(jnp.dot(smat, zj, preferred_element_type=F32))
        zn.append(jnp.sum(zj.astype(F32) * alt, axis=0, keepdims=True))
    for i in range(HY_BLOCKS):
        yr = yi = yn = None
        for j in range(HY_BLOCKS):
            d = i - j + HY_BLOCKS - 1
            hr = hr_ref[0, d]
            hi = hi_ref[0, d]
            pr = zr[j] * hr - zi[j] * hi
            pi = zr[j] * hi + zi[j] * hr
            pn = zn[j] * hn_ref[0, d]
            yr, yi, yn = (pr, pi, pn) if yr is None else (yr + pr, yi + pi, yn + pn)
        rows = slice(i * m, (i + 1) * m)
        y = (jnp.dot(cmat, yr.astype(BF16), preferred_element_type=F32)
             + jnp.dot(smat, yi.astype(BF16), preferred_element_type=F32)
             + alt * yn + bias_ref[0] * z_ref[0, rows, :].astype(F32))
        o_ref[0, rows, :] = (gate_ref[0, rows, :].astype(F32) * y).astype(o_ref.dtype)


def _hconv(zsrc, z_off, gsrc, g_off, bias, hr, hi, hn, order, cmat, smat, hy_w, tc=256):
    bsz, n, _ = zsrc.shape
    m = cmat.shape[0]
    nd = 2 * HY_BLOCKS - 1
    nt = hy_w // tc
    zo, go = z_off // tc, g_off // tc
    const = lambda shape: pl.BlockSpec(shape, lambda j, b: (0, 0), pipeline_mode=pl.Buffered(1))
    spec_blk = lambda r: pl.BlockSpec((1, nd, r, tc), lambda j, b: (order, 0, 0, j))
    return pl.pallas_call(
        _hconv_kernel,
        grid=(nt, bsz),
        in_specs=[pl.BlockSpec((1, n, tc), lambda j, b: (b, 0, zo + j)),
                  pl.BlockSpec((1, n, tc), lambda j, b: (b, 0, go + j)),
                  pl.BlockSpec((1, 1, tc), lambda j, b: (order, 0, j)),
                  spec_blk(m), spec_blk(m), spec_blk(1),
                  const((m, m)), const((m, m))],
        out_specs=pl.BlockSpec((1, n, tc), lambda j, b: (b, 0, j)),
        out_shape=jax.ShapeDtypeStruct((bsz, n, hy_w), BF16),
        compiler_params=_params(("parallel", "parallel")),
        name="hconv",
    )(zsrc, gsrc, bias, hr, hi, hn, cmat, smat)


def _mix_kernel(ys_ref, yh_ref, x_ref, gw_ref, gb_ref, gs_ref, gh_ref, wo_ref, g1_ref, o_ref, *, s5_w):
    gl = jnp.dot(ys_ref[0], gw_ref[...], preferred_element_type=F32) + gb_ref[...]
    a = gl[:, :s5_w] * jax.nn.sigmoid(gl[:, s5_w:])
    a = _rms(a, gs_ref[...]).astype(BF16)
    yh = _rms(yh_ref[0].astype(F32), gh_ref[...]).astype(BF16)
    proj = (jnp.dot(a, wo_ref[:s5_w, :], preferred_element_type=F32)
            + jnp.dot(yh, wo_ref[s5_w:, :], preferred_element_type=F32))
    o_ref[0] = x_ref[0] + g1_ref[0] * proj


def _mix(ys, yh, x, glu_w, glu_b, g_s5, g_hy, w_out, g1, tm=512):
    bsz, n, d = x.shape
    tm = min(tm, n)
    s5_w = ys.shape[-1]
    hy_w = yh.shape[-1]
    const = lambda shape: pl.BlockSpec(shape, lambda b, i: (0, 0), pipeline_mode=pl.Buffered(1))
    tok = lambda w: pl.BlockSpec((1, tm, w), lambda b, i: (b, i, 0))
    return pl.pallas_call(
        functools.partial(_mix_kernel, s5_w=s5_w),
        grid=(bsz, n // tm),
        in_specs=[tok(s5_w), tok(hy_w), tok(d),
                  const((s5_w, 2 * s5_w)), const((1, 2 * s5_w)), const((1, s5_w)), const((1, hy_w)),
                  const((s5_w + hy_w, d)),
                  pl.BlockSpec((1, 1, d), lambda b, i: (b, 0, 0))],
        out_specs=tok(d),
        out_shape=jax.ShapeDtypeStruct((bsz, n, d), F32),
        compiler_params=_params(("parallel", "parallel")),
        name="mix",
    )(ys, yh, x, glu_w, glu_b.reshape(1, -1), g_s5.reshape(1, -1), g_hy.reshape(1, -1), w_out, g1)


def _ffn_kernel(x_ref, g_ref, sh_ref, sc_ref, wg_ref, wu_ref, wd_ref, g2_ref, fg_ref, o_ref, h_scr, acc_scr):
    j = pl.program_id(2)

    @pl.when(j == 0)
    def _():
        h = _rms(x_ref[0], g_ref[...]) * (1.0 + sc_ref[0]) + sh_ref[0]
        h_scr[...] = h.astype(BF16)
        acc_scr[...] = jnp.zeros_like(acc_scr)

    h = h_scr[...]
    gate = jnp.dot(h, wg_ref[...], preferred_element_type=F32)
    up = jnp.dot(h, wu_ref[...], preferred_element_type=F32)
    act = (gate * jax.nn.sigmoid(gate) * up).astype(BF16)
    acc_scr[...] += jnp.dot(act, wd_ref[...], preferred_element_type=F32)

    @pl.when(j == pl.num_programs(2) - 1)
    def _():
        o_ref[0] = _rms(x_ref[0] + g2_ref[0] * acc_scr[...], fg_ref[...])


def _ffn(x, g, sh, sc, wg, wu, wd, g2, fg, tm=512, tf=512):
    bsz, n, d = x.shape
    tm = min(tm, n)
    dff = wg.shape[1]
    row = lambda: pl.BlockSpec((1, d), lambda b, i, j: (0, 0))
    per_b = lambda: pl.BlockSpec((1, 1, d), lambda b, i, j: (b, 0, 0))
    return pl.pallas_call(
        _ffn_kernel,
        grid=(bsz, n // tm, dff // tf),
        in_specs=[pl.BlockSpec((1, tm, d), lambda b, i, j: (b, i, 0)),
                  row(), per_b(), per_b(),
                  pl.BlockSpec((d, tf), lambda b, i, j: (0, j)),
                  pl.BlockSpec((d, tf), lambda b, i, j: (0, j)),
                  pl.BlockSpec((tf, d), lambda b, i, j: (j, 0)),
                  per_b(), row()],
        out_specs=pl.BlockSpec((1, tm, d), lambda b, i, j: (b, i, 0)),
        out_shape=jax.ShapeDtypeStruct((bsz, n, d), F32),
        scratch_shapes=[pltpu.VMEM((tm, d), BF16), pltpu.VMEM((tm, d), F32)],
        compiler_params=_params(("parallel", "parallel", "arbitrary")),
        name="ffn",
    )(x, g.reshape(1, d), sh, sc, wg, wu, wd, g2, fg.reshape(1, d))


def _positional_features(n):
    pos = jnp.abs(jnp.arange(2 * n, dtype=F32) - n)
    t = pos[:, None] / n
    bands = jnp.linspace(1e-4, HY_BANDS - 1, HY_BANDS, dtype=F32)
    ang = 2.0 * math.pi * pos[:, None] * bands[None, :] / n
    return jnp.concatenate([t, jnp.cos(ang), -jnp.sin(ang)], axis=-1)


def kernel(x, c, ctx, c_ctx, ada_w, ada_b, norm1_g, w_in, conv_w, conv_b, hy_w1, hy_b1, hy_w2, hy_b2, hy_w3,
           hy_sin_freq, hy_decay, hy_bias, s5_lam_re, s5_lam_im, s5_log_dt, s5_b_re, s5_b_im, s5_c_re, s5_c_im,
           s5_d, s5_glu_w, s5_glu_b, branch_g_s5, branch_g_hy, w_out, norm2_g, ffn_w_gate, ffn_w_up,
           ffn_w_down, final_g):
    bsz, n_lat, d = x.shape
    n_ctx = ctx.shape[1]
    assert ada_w.shape[0] == 1, "single-layer block"
    l = 0
    s5_w = s5_glu_w.shape[1]
    hy_w = w_in.shape[2] - s5_w
    hy_w //= 3
    groups = s5_w // S5_GROUP
    assert n_lat % S5_CHUNK == 0 and n_ctx % S5_CHUNK == 0 and n_lat % GRID_W == 0

    pad_rows = -(bsz + 1) % 8
    cc = jnp.concatenate([c, c_ctx[None, :], jnp.zeros((pad_rows, d), F32)], axis=0)
    mod = _ada(cc, ada_w[l], ada_b[l])
    sh1, sc1, g1, sh2, sc2, g2 = [mod[:bsz, None, i * d:(i + 1) * d] for i in range(6)]
    csh1, csc1 = [jnp.broadcast_to(mod[bsz, None, None, i * d:(i + 1) * d], (bsz, 1, d)) for i in range(2)]

    w_in_b = w_in[l].astype(BF16)
    ua, ub, p = _inproj(x, norm1_g[l], sh1, sc1, w_in_b, conv_w[l], conv_b[l], s5_w, tm=min(512, n_lat))
    uca, ucb = _inproj(ctx, norm1_g[l], csh1, csc1, w_in_b[:, :s5_w], None, None, s5_w, tm=min(256, n_ctx))

    wb, wc, m, ar, ai = _s5_prep(s5_lam_re[l], s5_lam_im[l], s5_log_dt[l], s5_b_re[l], s5_b_im[l],
                                 s5_c_re[l], s5_c_im[l], s5_d[l])
    flat = lambda u: u.reshape(u.shape[0], -1, 128)
    ys = _s5(flat(ua), flat(ub), flat(uca), flat(ucb), wb, wc, m, ar, ai,
             n_ctx // S5_CHUNK, n_lat // S5_CHUNK, bsz)

    z = _positional_features(n_lat)
    e_pad = -z.shape[1] % 128
    z = jnp.pad(z, ((0, 0), (0, e_pad)))
    w1 = jnp.pad(hy_w1[l], ((0, e_pad), (0, 0)))
    taps = _filters(z, w1, hy_b1[l], hy_w2[l], hy_b2[l], hy_w3[l], hy_sin_freq[l],
                    hy_decay[l].reshape(-1), hy_w)
    assert n_lat % (HY_BLOCKS * 128) == 0
    cmat, smat = _dft_tables(n_lat // HY_BLOCKS)
    hr, hi, hn = _spectrum(taps, cmat, smat)
    bias = hy_bias[l].reshape(HY_ORDER, 1, hy_w)
    z1 = _hconv(p, 0, p, hy_w, bias, hr, hi, hn, 0, cmat, smat, hy_w)
    yh = _hconv(z1, 0, p, 2 * hy_w, bias, hr, hi, hn, 1, cmat, smat, hy_w)

    x1 = _mix(ys, yh, x, s5_glu_w[l].astype(BF16), s5_glu_b[l], branch_g_s5[l], branch_g_hy[l],
              w_out[l].astype(BF16), g1)

    return _ffn(x1, norm2_g[l], sh2, sc2, ffn_w_gate[l].astype(BF16), ffn_w_up[l].astype(BF16),
                ffn_w_down[l].astype(BF16), g2, final_g)
```

```python
import functools
import math

import numpy as np
import jax
import jax.numpy as jnp
from jax import lax
from jax.experimental import pallas as pl
from jax.experimental.pallas import tpu as pltpu

EPS = 1e-6
GRID_W = 64
S5_GROUP = 16
S5_STATE = 64
S5_CHUNK = 16
HY_ORDER = 2
HY_BANDS = 16
V7X_VMEM_BYTES = 64 * 1024 * 1024
VMEM_LIMIT = 56 * 1024 * 1024

F32 = jnp.float32
BF16 = jnp.bfloat16
HIGHEST = lax.Precision.HIGHEST


def _params(sem):
    return pltpu.CompilerParams(dimension_semantics=sem, vmem_limit_bytes=VMEM_LIMIT)


def _rms(x, g):
    return x * lax.rsqrt(jnp.mean(x * x, axis=-1, keepdims=True) + EPS) * g


ROW_BLOCK = 16


def _by_rows(n_rows, fn):
    for r in range(0, n_rows, ROW_BLOCK):
        fn(slice(r, r + ROW_BLOCK))


def _norm_modulate_rows(x_ref, g_ref, sh_ref, sc_ref, h_ref):
    gain = g_ref[...] * (1.0 + sc_ref[0])
    shift = sh_ref[0]

    def block(rows):
        h_ref[rows, :] = (_rms(x_ref[0, rows, :], gain) + shift).astype(BF16)

    _by_rows(h_ref.shape[0], block)


def _ada_kernel(c_ref, w_ref, b_ref, o_ref):
    cv = c_ref[...]
    s = cv * jax.nn.sigmoid(cv)
    o_ref[...] = jnp.dot(s, w_ref[...], precision=HIGHEST, preferred_element_type=F32) + b_ref[...]


def _ada(cc, w, b, tn=1024):
    r, d = cc.shape
    n = w.shape[1]
    return pl.pallas_call(
        _ada_kernel,
        grid=(n // tn,),
        in_specs=[pl.BlockSpec((r, d), lambda j: (0, 0)),
                  pl.BlockSpec((d, tn), lambda j: (0, j)),
                  pl.BlockSpec((1, tn), lambda j: (0, j))],
        out_specs=pl.BlockSpec((r, tn), lambda j: (0, j)),
        out_shape=jax.ShapeDtypeStruct((r, n), F32),
        compiler_params=_params(("arbitrary",)),
        name="ada",
    )(cc, w, b.reshape(1, n))


def _gran_transpose(a):
    r = lax.broadcasted_iota(jnp.int32, a.shape, 1)
    q = lax.broadcasted_iota(jnp.int32, a.shape, 2) // S5_GROUP
    for dist in (4, 2, 1):
        rb = (r & dist) != 0
        qb = (q & dist) != 0
        up = pltpu.roll(pltpu.roll(a, 8 - dist, axis=1), S5_GROUP * dist, axis=2)
        dn = pltpu.roll(pltpu.roll(a, dist, axis=1), 128 - S5_GROUP * dist, axis=2)
        a = jnp.where(rb == qb, a, jnp.where(rb, dn, up))
    return a


def _inproj_kernel(*refs, s5_cols, tn, conv):
    if conv:
        x_ref, g_ref, sh_ref, sc_ref, w_ref, cw_ref, cb_ref, ua_ref, ub_ref, o_ref, h_scr = refs
    else:
        x_ref, g_ref, sh_ref, sc_ref, w_ref, ua_ref, ub_ref, h_scr = refs
    tm = x_ref.shape[1]
    _norm_modulate_rows(x_ref, g_ref, sh_ref, sc_ref, h_scr)

    for jn in range(s5_cols // tn):
        acc = jnp.dot(h_scr[...], w_ref[:, jn * tn:(jn + 1) * tn], preferred_element_type=F32)
        for cb in range(tn // 128):
            tile = acc[:, cb * 128:(cb + 1) * 128].reshape(tm // S5_CHUNK, 2, 8, 128)
            gb = jn * (tn // 128) + cb
            ua_ref[gb, :, 0] = _gran_transpose(tile[:, 0])
            ub_ref[gb, :, 0] = _gran_transpose(tile[:, 1])

    if conv:
        col = lax.broadcasted_iota(jnp.int32, (tm, tn), 0) % GRID_W
        for jn in range((w_ref.shape[1] - s5_cols) // tn):
            c0 = jn * tn
            acc = jnp.dot(h_scr[...], w_ref[:, s5_cols + c0:s5_cols + c0 + tn], preferred_element_type=F32)
            prev = jnp.where(col == 0, 0.0, pltpu.roll(acc, 1, axis=0))
            nxt = jnp.where(col == GRID_W - 1, 0.0, pltpu.roll(acc, tm - 1, axis=0))
            cw = cw_ref[:, c0:c0 + tn]
            o_ref[0, :, c0:c0 + tn] = (prev * cw[0:1] + acc * cw[1:2] + nxt * cw[2:3]
                                       + cb_ref[:, c0:c0 + tn]).astype(o_ref.dtype)


def _inproj(x, g, sh, sc, w, cw, cb, s5_cols, tm, tn=512):
    bsz, n, d = x.shape
    nn = w.shape[1]
    conv = cw is not None
    assert tm % GRID_W == 0 and n % tm == 0 and s5_cols % tn == 0 and (nn - s5_cols) % tn == 0
    ngb = s5_cols // 128
    const = lambda shape: pl.BlockSpec(shape, lambda b, i: (0,) * len(shape), pipeline_mode=pl.Buffered(1))
    per_b = pl.BlockSpec((1, 1, d), lambda b, i: (b, 0, 0))
    in_specs = [pl.BlockSpec((1, tm, d), lambda b, i: (b, i, 0)), const((1, d)), per_b, per_b, const((d, nn))]
    args = [x, g.reshape(1, d), sh, sc, w]
    u_spec = pl.BlockSpec((ngb, tm // S5_CHUNK, 1, 8, 128), lambda b, i: (0, i, b, 0, 0))
    u_shape = jax.ShapeDtypeStruct((ngb, n // S5_CHUNK, bsz, 8, 128), F32)
    out_specs, out_shape = [u_spec, u_spec], [u_shape, u_shape]
    if conv:
        in_specs += [const((3, nn - s5_cols)), const((1, nn - s5_cols))]
        args += [cw, cb.reshape(1, -1)]
        out_specs.append(pl.BlockSpec((1, tm, nn - s5_cols), lambda b, i: (b, i, 0)))
        out_shape.append(jax.ShapeDtypeStruct((bsz, n, nn - s5_cols), BF16))
    return pl.pallas_call(
        functools.partial(_inproj_kernel, s5_cols=s5_cols, tn=tn, conv=conv),
        grid=(bsz, n // tm),
        in_specs=in_specs,
        out_specs=out_specs,
        out_shape=out_shape,
        scratch_shapes=[pltpu.VMEM((tm, d), BF16)],
        compiler_params=_params(("parallel", "parallel")),
        name="inproj" if conv else "inproj_ctx",
    )(*args)


def _s5_prep(lam_re, lam_im, log_dt, b_re, b_im, c_re, c_im, d):
    t = S5_CHUNK
    lam_re, lam_im = lam_re.astype(F32), lam_im.astype(F32)
    dt = jnp.exp(log_dt.astype(F32))[..., None]
    k = jnp.arange(t + 1, dtype=F32)[:, None, None, None]
    mag = jnp.exp(lam_re * dt * k)
    pr, pi = mag * jnp.cos(lam_im * dt * k), mag * jnp.sin(lam_im * dt * k)
    er, ei = pr[1] - 1.0, pi[1]
    den = lam_re * lam_re + lam_im * lam_im
    qr, qi = (er * lam_re + ei * lam_im) / den, (ei * lam_re - er * lam_im) / den
    b_re, b_im = b_re.astype(F32), b_im.astype(F32)
    bbr = qr[..., None] * b_re - qi[..., None] * b_im
    bbi = qr[..., None] * b_im + qi[..., None] * b_re
    cr, ci = c_re.astype(F32), c_im.astype(F32)
    g = lam_re.shape[1]

    def times_b(powr, powi, dr):
        ar_, ai_ = powr.transpose(1, 0, 2)[:, :, None, :], powi.transpose(1, 0, 2)[:, :, None, :]
        br_, bi_ = bbr[dr].transpose(0, 2, 1)[:, None], bbi[dr].transpose(0, 2, 1)[:, None]
        return ar_ * br_ - ai_ * bi_, ar_ * bi_ + ai_ * br_

    wbf_r, wbf_i = times_b(pr[t - 1::-1, 0], pi[t - 1::-1, 0], 0)
    wbb_r, wbb_i = times_b(pr[:t, 1], pi[:t, 1], 1)
    wb = jnp.concatenate([wbf_r, wbb_r, wbf_i, wbb_i], axis=-1).reshape(g, t * S5_GROUP, 4 * S5_STATE)

    def c_times(powr, powi, dr):
        cr_, ci_ = cr[dr].transpose(0, 2, 1)[:, :, None, :], ci[dr].transpose(0, 2, 1)[:, :, None, :]
        ar_, ai_ = powr.transpose(1, 2, 0)[..., None], powi.transpose(1, 2, 0)[..., None]
        return cr_ * ar_ - ci_ * ai_, cr_ * ai_ + ci_ * ar_

    wcf_r, wcf_i = c_times(pr[1:, 0], pi[1:, 0], 0)
    wcb_r, wcb_i = c_times(pr[t:0:-1, 1], pi[t:0:-1, 1], 1)
    wc = jnp.concatenate([wcf_r, wcb_r, -wcf_i, -wcb_i], axis=1).reshape(g, 4 * S5_STATE, t * S5_GROUP)

    def lag_kernel(dr):
        abr = pr[:t, dr, :, :, None] * bbr[dr] - pi[:t, dr, :, :, None] * bbi[dr]
        abi = pr[:t, dr, :, :, None] * bbi[dr] + pi[:t, dr, :, :, None] * bbr[dr]
        return (jnp.einsum('kgpe,gcp->kgec', abr, cr[dr], precision=HIGHEST)
                - jnp.einsum('kgpe,gcp->kgec', abi, ci[dr], precision=HIGHEST))

    kf, kb = lag_kernel(0), lag_kernel(1)
    kf = kf.at[0].add(jnp.eye(S5_GROUP, dtype=F32) * d.astype(F32)[:, None, :])
    strip = jnp.concatenate([kb[:0:-1], (kf[0] + kb[0])[None], kf[1:]], axis=0)
    strip = strip.transpose(1, 2, 0, 3).reshape(g, S5_GROUP, (2 * t - 1) * S5_GROUP)
    m = jnp.pad(strip, ((0, 0), (0, 0), (0, S5_GROUP)))

    ar = jnp.concatenate([pr[t, 0], pr[t, 1]], axis=-1)[:, None, :]
    ai = jnp.concatenate([pi[t, 0], pi[t, 1]], axis=-1)[:, None, :]
    return wb.astype(BF16), wc.astype(BF16), m, ar, ai


def _gelu_tanh(x):
    return 0.5 * x * (1.0 + jnp.tanh(math.sqrt(2.0 / math.pi) * (x + 0.044715 * (x * x * x))))


S5_GB = 8
S5_PASS = 4


def _s5_kernel(ua_ref, ub_ref, uca_ref, ucb_ref, wb_ref, wc_ref, m_ref, ar_ref, ai_ref, o_ref,
               p_re, p_im, fw_re, fw_im, bw_re, bw_im, y_a, y_b, m_scr, u_scr, *, n_ctx, n_lat, bsz):
    half = 2 * S5_STATE
    r_ctx, r_lat = n_ctx * bsz, n_lat * bsz
    is_fwd = lax.broadcasted_iota(jnp.int32, (bsz, half), 1) < S5_STATE
    sel = lax.broadcasted_iota(jnp.int32, (r_lat, half), 1) < S5_STATE

    def load_u(a_ref, b_ref, g8, r):
        return jnp.concatenate([a_ref[0, pl.ds(g8, r, stride=S5_GB), :],
                                b_ref[0, pl.ds(g8, r, stride=S5_GB), :]], axis=1).astype(BF16)

    def rows(i):
        return pl.ds(pl.multiple_of(i * bsz, bsz), bsz)

    def one_pass(pass_idx, _):
        first = pass_idx * S5_PASS
        for q in range(S5_PASS):
            wbg = wb_ref[first + q]
            pc = jnp.dot(load_u(uca_ref, ucb_ref, first + q, r_ctx), wbg, preferred_element_type=F32)
            u_scr[q] = load_u(ua_ref, ub_ref, first + q, r_lat)
            pn = jnp.dot(u_scr[q], wbg, preferred_element_type=F32)
            p_re[q, 0:r_ctx, :] = pc[:, :half]
            p_im[q, 0:r_ctx, :] = pc[:, half:]
            p_re[q, r_ctx:, :] = pn[:, :half]
            p_im[q, r_ctx:, :] = pn[:, half:]

        ars = [jnp.broadcast_to(ar_ref[first + q], (bsz, half)) for q in range(S5_PASS)]
        ais = [jnp.broadcast_to(ai_ref[first + q], (bsz, half)) for q in range(S5_PASS)]

        def advance(carry, fi, bi):
            out = []
            for q in range(S5_PASS):
                s_re, s_im = carry[2 * q], carry[2 * q + 1]
                in_re = jnp.where(is_fwd, p_re[q, rows(fi), :], p_re[q, rows(bi), :])
                in_im = jnp.where(is_fwd, p_im[q, rows(fi), :], p_im[q, rows(bi), :])
                out += [ars[q] * s_re - ais[q] * s_im + in_re, ars[q] * s_im + ais[q] * s_re + in_im]
            return tuple(out)

        def ctx_step(k, carry):
            return advance(carry, k, n_ctx - 1 - k)

        def lat_step(k, carry):
            kb = n_lat - 1 - k
            for q in range(S5_PASS):
                fw_re[q, rows(k), :] = carry[2 * q]
                fw_im[q, rows(k), :] = carry[2 * q + 1]
                bw_re[q, rows(kb), :] = carry[2 * q]
                bw_im[q, rows(kb), :] = carry[2 * q + 1]
            return advance(carry, n_ctx + k, n_ctx + kb)

        carry = lax.fori_loop(0, n_ctx, ctx_step, (jnp.zeros((bsz, half), F32),) * (2 * S5_PASS))
        lax.fori_loop(0, n_lat, lat_step, carry)

        for q in range(S5_PASS):
            g8 = first + q
            ent = jnp.concatenate([jnp.where(sel, fw_re[q], bw_re[q]),
                                   jnp.where(sel, fw_im[q], bw_im[q])], axis=1).astype(BF16)
            strip = m_ref[g8]
            for sig in range(S5_CHUNK):
                lo = (S5_CHUNK - 1 - sig) * S5_GROUP
                m_scr[sig * S5_GROUP:(sig + 1) * S5_GROUP, :] = strip[:, lo:lo + S5_CHUNK * S5_GROUP].astype(BF16)
            y = (jnp.dot(u_scr[q], m_scr[...], preferred_element_type=F32)
                 + jnp.dot(ent, wc_ref[g8], preferred_element_type=F32))
            y = _gelu_tanh(y)
            y_a[pl.ds(g8, r_lat, stride=S5_GB), :] = y[:, :128]
            y_b[pl.ds(g8, r_lat, stride=S5_GB), :] = y[:, 128:]
        return 0

    lax.fori_loop(0, S5_GB // S5_PASS, one_pass, 0)

    cpb = math.gcd(n_lat, 16)

    def relayout(i, _):
        src = pl.ds(pl.multiple_of(i * (cpb * bsz * S5_GB), cpb * bsz * S5_GB), cpb * bsz * S5_GB)
        dst = pl.ds(pl.multiple_of(i * (cpb * S5_CHUNK), cpb * S5_CHUNK), cpb * S5_CHUNK)
        t_a = _gran_transpose(y_a[src, :].reshape(cpb * bsz, 8, 128)).reshape(cpb, bsz, 8, 128)
        t_b = _gran_transpose(y_b[src, :].reshape(cpb * bsz, 8, 128)).reshape(cpb, bsz, 8, 128)
        for b in range(bsz):
            tok = jnp.concatenate([t_a[:, b][:, None], t_b[:, b][:, None]], axis=1)
            o_ref[b, dst, :] = tok.reshape(cpb * S5_CHUNK, 128).astype(o_ref.dtype)
        return 0

    lax.fori_loop(0, n_lat // cpb, relayout, 0)


def _s5(ua, ub, uca, ucb, wb, wc, m, ar, ai, n_ctx, n_lat, bsz):
    ngb = ua.shape[0]
    half = 2 * S5_STATE
    w = S5_CHUNK * S5_GROUP
    r_ctx, r_lat = n_ctx * bsz, n_lat * bsz
    u_blk = lambda r: pl.BlockSpec((1, r * S5_GB, 128), lambda i: (i, 0, 0), pipeline_mode=pl.Buffered(1))
    w_blk = lambda shape: pl.BlockSpec((S5_GB,) + shape, lambda i: (i, 0, 0))
    pass_scr = lambda r: pltpu.VMEM((S5_PASS, r, half), F32)
    return pl.pallas_call(
        functools.partial(_s5_kernel, n_ctx=n_ctx, n_lat=n_lat, bsz=bsz),
        grid=(ngb,),
        in_specs=[u_blk(r_lat), u_blk(r_lat), u_blk(r_ctx), u_blk(r_ctx),
                  w_blk((w, 2 * half)), w_blk((2 * half, w)), w_blk(m.shape[1:]), w_blk((1, half)), w_blk((1, half))],
        out_specs=pl.BlockSpec((bsz, n_lat * S5_CHUNK, 128), lambda i: (0, 0, i)),
        out_shape=jax.ShapeDtypeStruct((bsz, n_lat * S5_CHUNK, ngb * 128), BF16),
        scratch_shapes=[pass_scr(r_ctx + r_lat)] * 2 + [pass_scr(r_lat)] * 4
                       + [pltpu.VMEM((r_lat * S5_GB, 128), F32)] * 2
                       + [pltpu.VMEM((w, w), BF16), pltpu.VMEM((S5_PASS, r_lat, w), BF16)],
        compiler_params=_params(("parallel",)),
        name="s5",
    )(ua, ub, uca, ucb, wb, wc, m, ar, ai)


def _split_bf16(v):
    hi = v.astype(BF16)
    return hi, (v - hi.astype(F32)).astype(BF16)


def _filt_kernel(z_ref, w1_ref, b1_ref, w2_ref, b2_ref, w3b_ref, w3f_ref, fr_ref, decb_ref, decf_ref, o_ref,
                 hi_scr, lo_scr):
    n = z_ref.shape[0]
    e = z_ref.shape[1] // 2
    f = hi_scr.shape[1] // 2

    @pl.when((pl.program_id(0) == 0) & (pl.program_id(1) == 0))
    def _():
        fr = fr_ref[...]
        h = jnp.sin(fr[0:1] * (jnp.dot(z_ref[...], w1_ref[...], precision=HIGHEST, preferred_element_type=F32)
                               + b1_ref[...]))
        h = jnp.sin(fr[1:2] * (jnp.dot(h, w2_ref[...], precision=HIGHEST, preferred_element_type=F32)
                               + b2_ref[...]))
        hi_scr[...], lo_scr[...] = _split_bf16(h)

    def last_layer(cols, w_ref):
        w_hi, w_lo = _split_bf16(w_ref[...])
        h_hi = hi_scr[:, cols]
        return (jnp.dot(h_hi, w_hi, preferred_element_type=F32) + jnp.dot(h_hi, w_lo, preferred_element_type=F32)
                + jnp.dot(lo_scr[:, cols], w_hi, preferred_element_type=F32))

    bwd = last_layer(slice(0, f), w3b_ref) * jnp.exp(-z_ref[:, 0:1] * jnp.abs(decb_ref[...]))
    bwd = jnp.where(lax.broadcasted_iota(jnp.int32, bwd.shape, 0) == 0, 0.0, bwd)
    fwd = last_layer(slice(f, 2 * f), w3f_ref) * jnp.exp(-z_ref[:, e:e + 1] * jnp.abs(decf_ref[...]))
    den = jnp.sum(jnp.abs(bwd), axis=0, keepdims=True) + jnp.sum(jnp.abs(fwd), axis=0, keepdims=True) + EPS
    o_ref[0, 0:n, :] = bwd / den
    o_ref[0, n:, :] = fwd / den


def _filters(z2, w1, b1, w2, b2, w3, freq, decay, hy_w, tn=512):
    n2, e = z2.shape
    n = n2 // 2
    f = w1.shape[1]
    nt = hy_w // tn
    zz = jnp.concatenate([z2[:n], z2[n:]], axis=1)
    twice = lambda w: jnp.concatenate([jnp.pad(w, ((0, 0), (0, w.shape[1]))), jnp.pad(w, ((0, 0), (w.shape[1], 0)))])
    pair = lambda v: jnp.concatenate([v, v], axis=-1)
    full = lambda shape: pl.BlockSpec(shape, lambda o, j: (0, 0))
    fwd_col = lambda r: pl.BlockSpec((r, tn), lambda o, j: (0, o * 2 * nt + j))
    bwd_col = lambda r: pl.BlockSpec((r, tn), lambda o, j: (0, o * 2 * nt + nt + j))
    dec = decay.reshape(1, -1)
    return pl.pallas_call(
        _filt_kernel,
        grid=(HY_ORDER, nt),
        in_specs=[full((n, 2 * e)), full((2 * e, 2 * f)), full((1, 2 * f)), full((2 * f, 2 * f)), full((1, 2 * f)),
                  bwd_col(f), fwd_col(f), full((2, 2 * f)), bwd_col(1), fwd_col(1)],
        out_specs=pl.BlockSpec((1, n2, tn), lambda o, j: (o, 0, j)),
        out_shape=jax.ShapeDtypeStruct((HY_ORDER, n2, hy_w), F32),
        scratch_shapes=[pltpu.VMEM((n, 2 * f), BF16)] * 2,
        compiler_params=_params(("arbitrary", "arbitrary")),
        name="filt",
    )(zz, twice(w1), pair(b1.reshape(1, f)), twice(w2), pair(b2.reshape(1, f)), w3, w3, pair(freq), dec, dec)


HY_BLOCKS = 4


def _dft_tables(n):
    k = np.arange(n, dtype=np.int64)
    ang = (np.outer(k, k) % (2 * n)).astype(np.float64) * (math.pi / n)
    return jnp.asarray(np.cos(ang), dtype=F32).astype(BF16), jnp.asarray(-np.sin(ang), dtype=F32).astype(BF16)


def _alt(n):
    return jnp.where(lax.broadcasted_iota(jnp.int32, (n, 1), 0) % 2 == 0, 1.0, -1.0).astype(F32)


def _spec_kernel(h_ref, c_ref, s_ref, hr_ref, hi_ref, hn_ref):
    m = c_ref.shape[0]
    cmat = c_ref[...]
    smat = s_ref[...]
    alt = _alt(m)
    wk = jnp.where(lax.broadcasted_iota(jnp.int32, (m, 1), 0) == 0, 0.5 / m, 1.0 / m)
    prev = None
    for j in range(2 * HY_BLOCKS):
        blk = h_ref[0, j * m:(j + 1) * m, :]
        bb = blk.astype(BF16)
        cur = (jnp.dot(cmat, bb, preferred_element_type=F32), jnp.dot(smat, bb, preferred_element_type=F32),
               jnp.sum(blk * alt, axis=0, keepdims=True), blk[0:1])
        if prev is not None:
            hr_ref[0, j - 1] = (cur[0] + alt * (prev[0] - prev[3])) * wk
            hi_ref[0, j - 1] = (cur[1] + alt * prev[1]) * wk
            hn_ref[0, j - 1] = (cur[2] + prev[2] - prev[3]) * (0.5 / m)
        prev = cur


def _spectrum(taps, cmat, smat, tn=256):
    _, n2, hy_w = taps.shape
    m = cmat.shape[0]
    nd = 2 * HY_BLOCKS - 1
    const = lambda shape: pl.BlockSpec(shape, lambda o, j: (0, 0), pipeline_mode=pl.Buffered(1))
    out_blk = lambda r: pl.BlockSpec((1, nd, r, tn), lambda o, j: (o, 0, 0, j))
    return pl.pallas_call(
        _spec_kernel,
        grid=(HY_ORDER, hy_w // tn),
        in_specs=[pl.BlockSpec((1, n2, tn), lambda o, j: (o, 0, j)), const((m, m)), const((m, m))],
        out_specs=[out_blk(m), out_blk(m), out_blk(1)],
        out_shape=[jax.ShapeDtypeStruct((HY_ORDER, nd, m, hy_w), F32),
                   jax.ShapeDtypeStruct((HY_ORDER, nd, m, hy_w), F32),
                   jax.ShapeDtypeStruct((HY_ORDER, nd, 1, hy_w), F32)],
        compiler_params=_params(("parallel", "parallel")),
        name="spec",
    )(taps, cmat, smat)


def _hconv_kernel(z_ref, gate_ref, bias_ref, hr_ref, hi_ref, hn_ref, c_ref, s_ref, o_ref):
    m = c_ref.shape[0]
    cmat = c_ref[...]
    smat = s_ref[...]
    alt = _alt(m)
    zr, zi, zn = [], [], []
    for j in range(HY_BLOCKS):
        zj = z_ref[0, j * m:(j + 1) * m, :]
        zr.append(jnp.dot(cmat, zj, preferred_element_type=F32))
        zi.append(jnp.dot(smat, zj, preferred_element_type=F32))
        zn.append(jnp.sum(zj.astype(F32) * alt, axis=0, keepdims=True))
    for i in range(HY_BLOCKS):
        yr = yi = yn = None
        for j in range(HY_BLOCKS):
            d = i - j + HY_BLOCKS - 1
            hr = hr_ref[0, d]
            hi = hi_ref[0, d]
            pr = zr[j] * hr - zi[j] * hi
            pi = zr[j] * hi + zi[j] * hr
            pn = zn[j] * hn_ref[0, d]
            yr, yi, yn = (pr, pi, pn) if yr is None else (yr + pr, yi + pi, yn + pn)
        rows = slice(i * m, (i + 1) * m)
        y = (jnp.dot(cmat, yr.astype(BF16), preferred_element_type=F32)
             + jnp.dot(smat, yi.astype(BF16), preferred_element_type=F32)
             + alt * yn + bias_ref[0] * z_ref[0, rows, :].astype(F32))
        o_ref[0, rows, :] = (gate_ref[0, rows, :].astype(F32) * y).astype(o_ref.dtype)


def _hconv(zsrc, z_off, gsrc, g_off, bias, hr, hi, hn, order, cmat, smat, hy_w, tc=256):
    bsz, n, _ = zsrc.shape
    m = cmat.shape[0]
    nd = 2 * HY_BLOCKS - 1
    nt = hy_w // tc
    zo, go = z_off // tc, g_off // tc
    const = lambda shape: pl.BlockSpec(shape, lambda j, b: (0, 0), pipeline_mode=pl.Buffered(1))
    spec_blk = lambda r: pl.BlockSpec((1, nd, r, tc), lambda j, b: (order, 0, 0, j))
    return pl.pallas_call(
        _hconv_kernel,
        grid=(nt, bsz),
        in_specs=[pl.BlockSpec((1, n, tc), lambda j, b: (b, 0, zo + j)),
                  pl.BlockSpec((1, n, tc), lambda j, b: (b, 0, go + j)),
                  pl.BlockSpec((1, 1, tc), lambda j, b: (order, 0, j)),
                  spec_blk(m), spec_blk(m), spec_blk(1),
                  const((m, m)), const((m, m))],
        out_specs=pl.BlockSpec((1, n, tc), lambda j, b: (b, 0, j)),
        out_shape=jax.ShapeDtypeStruct((bsz, n, hy_w), BF16),
        compiler_params=_params(("parallel", "parallel")),
        name="hconv",
    )(zsrc, gsrc, bias, hr, hi, hn, cmat, smat)


def _mix_kernel(ys_ref, yh_ref, x_ref, gw_ref, gb_ref, gs_ref, gh_ref, wo_ref, g1_ref, o_ref, *, s5_w):
    gl = jnp.dot(ys_ref[0], gw_ref[...], preferred_element_type=F32) + gb_ref[...]
    a = gl[:, :s5_w] * jax.nn.sigmoid(gl[:, s5_w:])
    a = _rms(a, gs_ref[...]).astype(BF16)
    yh = _rms(yh_ref[0].astype(F32), gh_ref[...]).astype(BF16)
    proj = (jnp.dot(a, wo_ref[:s5_w, :], preferred_element_type=F32)
            + jnp.dot(yh, wo_ref[s5_w:, :], preferred_element_type=F32))
    o_ref[0] = x_ref[0] + g1_ref[0] * proj


def _mix(ys, yh, x, glu_w, glu_b, g_s5, g_hy, w_out, g1, tm=512):
    bsz, n, d = x.shape
    tm = min(tm, n)
    s5_w = ys.shape[-1]
    hy_w = yh.shape[-1]
    const = lambda shape: pl.BlockSpec(shape, lambda b, i: (0, 0), pipeline_mode=pl.Buffered(1))
    tok = lambda w: pl.BlockSpec((1, tm, w), lambda b, i: (b, i, 0))
    return pl.pallas_call(
        functools.partial(_mix_kernel, s5_w=s5_w),
        grid=(bsz, n // tm),
        in_specs=[tok(s5_w), tok(hy_w), tok(d),
                  const((s5_w, 2 * s5_w)), const((1, 2 * s5_w)), const((1, s5_w)), const((1, hy_w)),
                  const((s5_w + hy_w, d)),
                  pl.BlockSpec((1, 1, d), lambda b, i: (b, 0, 0))],
        out_specs=tok(d),
        out_shape=jax.ShapeDtypeStruct((bsz, n, d), F32),
        compiler_params=_params(("parallel", "parallel")),
        name="mix",
    )(ys, yh, x, glu_w, glu_b.reshape(1, -1), g_s5.reshape(1, -1), g_hy.reshape(1, -1), w_out, g1)


def _ffn_kernel(x_ref, g_ref, sh_ref, sc_ref, wg_ref, wu_ref, wd_ref, g2_ref, fg_ref, o_ref, h_scr, acc_scr):
    j = pl.program_id(2)

    @pl.when(j == 0)
    def _():
        _norm_modulate_rows(x_ref, g_ref, sh_ref, sc_ref, h_scr)
        acc_scr[...] = jnp.zeros_like(acc_scr)

    h = h_scr[...]
    gate = jnp.dot(h, wg_ref[...], preferred_element_type=F32)
    up = jnp.dot(h, wu_ref[...], preferred_element_type=F32)
    act = (gate * jax.nn.sigmoid(gate) * up).astype(BF16)
    acc_scr[...] += jnp.dot(act, wd_ref[...], preferred_element_type=F32)

    @pl.when(j == pl.num_programs(2) - 1)
    def _():
        gate2 = g2_ref[0]
        final_gain = fg_ref[...]

        def block(rows):
            o_ref[0, rows, :] = _rms(x_ref[0, rows, :] + gate2 * acc_scr[rows, :], final_gain)

        _by_rows(acc_scr.shape[0], block)


def _ffn(x, g, sh, sc, wg, wu, wd, g2, fg, tm=512, tf=512):
    bsz, n, d = x.shape
    tm = min(tm, n)
    dff = wg.shape[1]
    row = lambda: pl.BlockSpec((1, d), lambda b, i, j: (0, 0))
    per_b = lambda: pl.BlockSpec((1, 1, d), lambda b, i, j: (b, 0, 0))
    return pl.pallas_call(
        _ffn_kernel,
        grid=(bsz, n // tm, dff // tf),
        in_specs=[pl.BlockSpec((1, tm, d), lambda b, i, j: (b, i, 0)),
                  row(), per_b(), per_b(),
                  pl.BlockSpec((d, tf), lambda b, i, j: (0, j)),
                  pl.BlockSpec((d, tf), lambda b, i, j: (0, j)),
                  pl.BlockSpec((tf, d), lambda b, i, j: (j, 0)),
                  per_b(), row()],
        out_specs=pl.BlockSpec((1, tm, d), lambda b, i, j: (b, i, 0)),
        out_shape=jax.ShapeDtypeStruct((bsz, n, d), F32),
        scratch_shapes=[pltpu.VMEM((tm, d), BF16), pltpu.VMEM((tm, d), F32)],
        compiler_params=_params(("parallel", "parallel", "arbitrary")),
        name="ffn",
    )(x, g.reshape(1, d), sh, sc, wg, wu, wd, g2, fg.reshape(1, d))


def _positional_features(n):
    pos = jnp.abs(jnp.arange(2 * n, dtype=F32) - n)
    t = pos[:, None] / n
    bands = jnp.linspace(1e-4, HY_BANDS - 1, HY_BANDS, dtype=F32)
    ang = 2.0 * math.pi * pos[:, None] * bands[None, :] / n
    return jnp.concatenate([t, jnp.cos(ang), -jnp.sin(ang)], axis=-1)


def kernel(x, c, ctx, c_ctx, ada_w, ada_b, norm1_g, w_in, conv_w, conv_b, hy_w1, hy_b1, hy_w2, hy_b2, hy_w3,
           hy_sin_freq, hy_decay, hy_bias, s5_lam_re, s5_lam_im, s5_log_dt, s5_b_re, s5_b_im, s5_c_re, s5_c_im,
           s5_d, s5_glu_w, s5_glu_b, branch_g_s5, branch_g_hy, w_out, norm2_g, ffn_w_gate, ffn_w_up,
           ffn_w_down, final_g):
    bsz, n_lat, d = x.shape
    n_ctx = ctx.shape[1]
    assert ada_w.shape[0] == 1, "single-layer block"
    l = 0
    s5_w = s5_glu_w.shape[1]
    hy_w = w_in.shape[2] - s5_w
    hy_w //= 3
    groups = s5_w // S5_GROUP
    assert n_lat % S5_CHUNK == 0 and n_ctx % S5_CHUNK == 0 and n_lat % GRID_W == 0

    pad_rows = -(bsz + 1) % 8
    cc = jnp.concatenate([c, c_ctx[None, :], jnp.zeros((pad_rows, d), F32)], axis=0)
    mod = _ada(cc, ada_w[l], ada_b[l])
    sh1, sc1, g1, sh2, sc2, g2 = [mod[:bsz, None, i * d:(i + 1) * d] for i in range(6)]
    csh1, csc1 = [jnp.broadcast_to(mod[bsz, None, None, i * d:(i + 1) * d], (bsz, 1, d)) for i in range(2)]

    w_in_b = w_in[l].astype(BF16)
    ua, ub, p = _inproj(x, norm1_g[l], sh1, sc1, w_in_b, conv_w[l], conv_b[l], s5_w, tm=min(512, n_lat))
    uca, ucb = _inproj(ctx, norm1_g[l], csh1, csc1, w_in_b[:, :s5_w], None, None, s5_w, tm=min(256, n_ctx))

    wb, wc, m, ar, ai = _s5_prep(s5_lam_re[l], s5_lam_im[l], s5_log_dt[l], s5_b_re[l], s5_b_im[l],
                                 s5_c_re[l], s5_c_im[l], s5_d[l])
    flat = lambda u: u.reshape(u.shape[0], -1, 128)
    ys = _s5(flat(ua), flat(ub), flat(uca), flat(ucb), wb, wc, m, ar, ai,
             n_ctx // S5_CHUNK, n_lat // S5_CHUNK, bsz)

    z = _positional_features(n_lat)
    e_pad = -z.shape[1] % 128
    z = jnp.pad(z, ((0, 0), (0, e_pad)))
    w1 = jnp.pad(hy_w1[l], ((0, e_pad), (0, 0)))
    taps = _filters(z, w1, hy_b1[l], hy_w2[l], hy_b2[l], hy_w3[l], hy_sin_freq[l],
                    hy_decay[l].reshape(-1), hy_w)
    assert n_lat % (HY_BLOCKS * 128) == 0
    cmat, smat = _dft_tables(n_lat // HY_BLOCKS)
    hr, hi, hn = _spectrum(taps, cmat, smat)
    bias = hy_bias[l].reshape(HY_ORDER, 1, hy_w)
    z1 = _hconv(p, 0, p, hy_w, bias, hr, hi, hn, 0, cmat, smat, hy_w)
    yh = _hconv(z1, 0, p, 2 * hy_w, bias, hr, hi, hn, 1, cmat, smat, hy_w)

    x1 = _mix(ys, yh, x, s5_glu_w[l].astype(BF16), s5_glu_b[l], branch_g_s5[l], branch_g_hy[l],
              w_out[l].astype(BF16), g1)

    return _ffn(x1, norm2_g[l], sh2, sc2, ffn_w_gate[l].astype(BF16), ffn_w_up[l].astype(BF16),
                ffn_w_down[l].astype(BF16), g2, final_g)
```

```python
import functools
import math

import numpy as np
import jax
import jax.numpy as jnp
from jax import lax
from jax.experimental import pallas as pl
from jax.experimental.pallas import tpu as pltpu

EPS = 1e-6
GRID_W = 64
S5_GROUP = 16
S5_STATE = 64
S5_CHUNK = 16
HY_ORDER = 2
HY_BANDS = 16
V7X_VMEM_BYTES = 64 * 1024 * 1024
VMEM_LIMIT = 56 * 1024 * 1024

F32 = jnp.float32
BF16 = jnp.bfloat16
HIGHEST = lax.Precision.HIGHEST


def _params(sem):
    return pltpu.CompilerParams(dimension_semantics=sem, vmem_limit_bytes=VMEM_LIMIT)


def _rms(x, g):
    return x * lax.rsqrt(jnp.mean(x * x, axis=-1, keepdims=True) + EPS) * g


ROW_BLOCK = 16


def _by_rows(n_rows, fn):
    for r in range(0, n_rows, ROW_BLOCK):
        fn(slice(r, r + ROW_BLOCK))


def _norm_modulate_rows(x_ref, g_ref, sh_ref, sc_ref, h_ref):
    gain = g_ref[...] * (1.0 + sc_ref[0])
    shift = sh_ref[0]

    def block(rows):
        h_ref[rows, :] = (_rms(x_ref[0, rows, :], gain) + shift).astype(BF16)

    _by_rows(h_ref.shape[0], block)


def _ada_kernel(c_ref, w_ref, b_ref, o_ref):
    cv = c_ref[...]
    s = cv * jax.nn.sigmoid(cv)
    o_ref[...] = jnp.dot(s, w_ref[...], precision=HIGHEST, preferred_element_type=F32) + b_ref[...]


def _ada(cc, w, b, tn=1024):
    r, d = cc.shape
    n = w.shape[1]
    return pl.pallas_call(
        _ada_kernel,
        grid=(n // tn,),
        in_specs=[pl.BlockSpec((r, d), lambda j: (0, 0)),
                  pl.BlockSpec((d, tn), lambda j: (0, j)),
                  pl.BlockSpec((1, tn), lambda j: (0, j))],
        out_specs=pl.BlockSpec((r, tn), lambda j: (0, j)),
        out_shape=jax.ShapeDtypeStruct((r, n), F32),
        compiler_params=_params(("arbitrary",)),
        name="ada",
    )(cc, w, b.reshape(1, n))


def _gran_transpose(a):
    r = lax.broadcasted_iota(jnp.int32, a.shape, 1)
    q = lax.broadcasted_iota(jnp.int32, a.shape, 2) // S5_GROUP
    for dist in (4, 2, 1):
        rb = (r & dist) != 0
        qb = (q & dist) != 0
        up = pltpu.roll(pltpu.roll(a, 8 - dist, axis=1), S5_GROUP * dist, axis=2)
        dn = pltpu.roll(pltpu.roll(a, dist, axis=1), 128 - S5_GROUP * dist, axis=2)
        a = jnp.where(rb == qb, a, jnp.where(rb, dn, up))
    return a


def _inproj_kernel(*refs, s5_cols, tn, conv):
    if conv:
        x_ref, g_ref, sh_ref, sc_ref, w_ref, cw_ref, cb_ref, ua_ref, ub_ref, o_ref, h_scr = refs
    else:
        x_ref, g_ref, sh_ref, sc_ref, w_ref, ua_ref, ub_ref, h_scr = refs
    tm = x_ref.shape[1]
    _norm_modulate_rows(x_ref, g_ref, sh_ref, sc_ref, h_scr)

    for jn in range(s5_cols // tn):
        acc = jnp.dot(h_scr[...], w_ref[:, jn * tn:(jn + 1) * tn], preferred_element_type=F32)
        for cb in range(tn // 128):
            tile = acc[:, cb * 128:(cb + 1) * 128].reshape(tm // S5_CHUNK, 2, 8, 128)
            gb = jn * (tn // 128) + cb
            ua_ref[gb, :, 0] = _gran_transpose(tile[:, 0])
            ub_ref[gb, :, 0] = _gran_transpose(tile[:, 1])

    if conv:
        col = lax.broadcasted_iota(jnp.int32, (tm, tn), 0) % GRID_W
        for jn in range((w_ref.shape[1] - s5_cols) // tn):
            c0 = jn * tn
            acc = jnp.dot(h_scr[...], w_ref[:, s5_cols + c0:s5_cols + c0 + tn], preferred_element_type=F32)
            prev = jnp.where(col == 0, 0.0, pltpu.roll(acc, 1, axis=0))
            nxt = jnp.where(col == GRID_W - 1, 0.0, pltpu.roll(acc, tm - 1, axis=0))
            cw = cw_ref[:, c0:c0 + tn]
            o_ref[0, :, c0:c0 + tn] = (prev * cw[0:1] + acc * cw[1:2] + nxt * cw[2:3]
                                       + cb_ref[:, c0:c0 + tn]).astype(o_ref.dtype)


def _inproj(x, g, sh, sc, w, cw, cb, s5_cols, tm, tn=512):
    bsz, n, d = x.shape
    nn = w.shape[1]
    conv = cw is not None
    assert tm % GRID_W == 0 and n % tm == 0 and s5_cols % tn == 0 and (nn - s5_cols) % tn == 0
    ngb = s5_cols // 128
    const = lambda shape: pl.BlockSpec(shape, lambda b, i: (0,) * len(shape), pipeline_mode=pl.Buffered(1))
    per_b = pl.BlockSpec((1, 1, d), lambda b, i: (b, 0, 0))
    in_specs = [pl.BlockSpec((1, tm, d), lambda b, i: (b, i, 0)), const((1, d)), per_b, per_b, const((d, nn))]
    args = [x, g.reshape(1, d), sh, sc, w]
    u_spec = pl.BlockSpec((ngb, tm // S5_CHUNK, 1, 8, 128), lambda b, i: (0, i, b, 0, 0))
    u_shape = jax.ShapeDtypeStruct((ngb, n // S5_CHUNK, bsz, 8, 128), F32)
    out_specs, out_shape = [u_spec, u_spec], [u_shape, u_shape]
    if conv:
        in_specs += [const((3, nn - s5_cols)), const((1, nn - s5_cols))]
        args += [cw, cb.reshape(1, -1)]
        out_specs.append(pl.BlockSpec((1, tm, nn - s5_cols), lambda b, i: (b, i, 0)))
        out_shape.append(jax.ShapeDtypeStruct((bsz, n, nn - s5_cols), BF16))
    return pl.pallas_call(
        functools.partial(_inproj_kernel, s5_cols=s5_cols, tn=tn, conv=conv),
        grid=(bsz, n // tm),
        in_specs=in_specs,
        out_specs=out_specs,
        out_shape=out_shape,
        scratch_shapes=[pltpu.VMEM((tm, d), BF16)],
        compiler_params=_params(("parallel", "parallel")),
        name="inproj" if conv else "inproj_ctx",
    )(*args)


def _s5_prep(lam_re, lam_im, log_dt, b_re, b_im, c_re, c_im, d):
    t = S5_CHUNK
    lam_re, lam_im = lam_re.astype(F32), lam_im.astype(F32)
    dt = jnp.exp(log_dt.astype(F32))[..., None]
    k = jnp.arange(t + 1, dtype=F32)[:, None, None, None]
    mag = jnp.exp(lam_re * dt * k)
    pr, pi = mag * jnp.cos(lam_im * dt * k), mag * jnp.sin(lam_im * dt * k)
    er, ei = pr[1] - 1.0, pi[1]
    den = lam_re * lam_re + lam_im * lam_im
    qr, qi = (er * lam_re + ei * lam_im) / den, (ei * lam_re - er * lam_im) / den
    b_re, b_im = b_re.astype(F32), b_im.astype(F32)
    bbr = qr[..., None] * b_re - qi[..., None] * b_im
    bbi = qr[..., None] * b_im + qi[..., None] * b_re
    cr, ci = c_re.astype(F32), c_im.astype(F32)
    g = lam_re.shape[1]

    def times_b(powr, powi, dr):
        ar_, ai_ = powr.transpose(1, 0, 2)[:, :, None, :], powi.transpose(1, 0, 2)[:, :, None, :]
        br_, bi_ = bbr[dr].transpose(0, 2, 1)[:, None], bbi[dr].transpose(0, 2, 1)[:, None]
        return ar_ * br_ - ai_ * bi_, ar_ * bi_ + ai_ * br_

    wbf_r, wbf_i = times_b(pr[t - 1::-1, 0], pi[t - 1::-1, 0], 0)
    wbb_r, wbb_i = times_b(pr[:t, 1], pi[:t, 1], 1)
    wb = jnp.concatenate([wbf_r, wbb_r, wbf_i, wbb_i], axis=-1).reshape(g, t * S5_GROUP, 4 * S5_STATE)

    def c_times(powr, powi, dr):
        cr_, ci_ = cr[dr].transpose(0, 2, 1)[:, :, None, :], ci[dr].transpose(0, 2, 1)[:, :, None, :]
        ar_, ai_ = powr.transpose(1, 2, 0)[..., None], powi.transpose(1, 2, 0)[..., None]
        return cr_ * ar_ - ci_ * ai_, cr_ * ai_ + ci_ * ar_

    wcf_r, wcf_i = c_times(pr[1:, 0], pi[1:, 0], 0)
    wcb_r, wcb_i = c_times(pr[t:0:-1, 1], pi[t:0:-1, 1], 1)
    wc = jnp.concatenate([wcf_r, wcb_r, -wcf_i, -wcb_i], axis=1).reshape(g, 4 * S5_STATE, t * S5_GROUP)

    def lag_kernel(dr):
        abr = pr[:t, dr, :, :, None] * bbr[dr] - pi[:t, dr, :, :, None] * bbi[dr]
        abi = pr[:t, dr, :, :, None] * bbi[dr] + pi[:t, dr, :, :, None] * bbr[dr]
        return (jnp.einsum('kgpe,gcp->kgec', abr, cr[dr], precision=HIGHEST)
                - jnp.einsum('kgpe,gcp->kgec', abi, ci[dr], precision=HIGHEST))

    kf, kb = lag_kernel(0), lag_kernel(1)
    kf = kf.at[0].add(jnp.eye(S5_GROUP, dtype=F32) * d.astype(F32)[:, None, :])
    strip = jnp.concatenate([kb[:0:-1], (kf[0] + kb[0])[None], kf[1:]], axis=0)
    strip = strip.transpose(1, 2, 0, 3).reshape(g, S5_GROUP, (2 * t - 1) * S5_GROUP)
    m = jnp.pad(strip, ((0, 0), (0, 0), (0, S5_GROUP)))

    ar = jnp.concatenate([pr[t, 0], pr[t, 1]], axis=-1)[:, None, :]
    ai = jnp.concatenate([pi[t, 0], pi[t, 1]], axis=-1)[:, None, :]
    return wb.astype(BF16), wc.astype(BF16), m, ar, ai


def _gelu_tanh(x):
    return 0.5 * x * (1.0 + jnp.tanh(math.sqrt(2.0 / math.pi) * (x + 0.044715 * (x * x * x))))


S5_GB = 8
S5_PASS = 4


def _s5_kernel(ua_ref, ub_ref, uca_ref, ucb_ref, wb_ref, wc_ref, m_ref, ar_ref, ai_ref, o_ref,
               p_re, p_im, fw_re, fw_im, bw_re, bw_im, y_a, y_b, m_scr, u_scr, *, n_ctx, n_lat, bsz):
    half = 2 * S5_STATE
    r_ctx, r_lat = n_ctx * bsz, n_lat * bsz
    is_fwd = lax.broadcasted_iota(jnp.int32, (bsz, half), 1) < S5_STATE
    sel = lax.broadcasted_iota(jnp.int32, (r_lat, half), 1) < S5_STATE

    def load_u(a_ref, b_ref, g8, r):
        return jnp.concatenate([a_ref[0, pl.ds(g8, r, stride=S5_GB), :],
                                b_ref[0, pl.ds(g8, r, stride=S5_GB), :]], axis=1).astype(BF16)

    def rows(i):
        return pl.ds(pl.multiple_of(i * bsz, bsz), bsz)

    def one_pass(pass_idx, _):
        first = pass_idx * S5_PASS
        for q in range(S5_PASS):
            wbg = wb_ref[first + q]
            pc = jnp.dot(load_u(uca_ref, ucb_ref, first + q, r_ctx), wbg, preferred_element_type=F32)
            u_scr[q] = load_u(ua_ref, ub_ref, first + q, r_lat)
            pn = jnp.dot(u_scr[q], wbg, preferred_element_type=F32)
            p_re[q, 0:r_ctx, :] = pc[:, :half]
            p_im[q, 0:r_ctx, :] = pc[:, half:]
            p_re[q, r_ctx:, :] = pn[:, :half]
            p_im[q, r_ctx:, :] = pn[:, half:]

        ars = [jnp.broadcast_to(ar_ref[first + q], (bsz, half)) for q in range(S5_PASS)]
        ais = [jnp.broadcast_to(ai_ref[first + q], (bsz, half)) for q in range(S5_PASS)]

        def advance(carry, fi, bi):
            out = []
            for q in range(S5_PASS):
                s_re, s_im = carry[2 * q], carry[2 * q + 1]
                in_re = jnp.where(is_fwd, p_re[q, rows(fi), :], p_re[q, rows(bi), :])
                in_im = jnp.where(is_fwd, p_im[q, rows(fi), :], p_im[q, rows(bi), :])
                out += [ars[q] * s_re - ais[q] * s_im + in_re, ars[q] * s_im + ais[q] * s_re + in_im]
            return tuple(out)

        def ctx_step(k, carry):
            return advance(carry, k, n_ctx - 1 - k)

        def lat_step(k, carry):
            kb = n_lat - 1 - k
            for q in range(S5_PASS):
                fw_re[q, rows(k), :] = carry[2 * q]
                fw_im[q, rows(k), :] = carry[2 * q + 1]
                bw_re[q, rows(kb), :] = carry[2 * q]
                bw_im[q, rows(kb), :] = carry[2 * q + 1]
            return advance(carry, n_ctx + k, n_ctx + kb)

        carry = lax.fori_loop(0, n_ctx, ctx_step, (jnp.zeros((bsz, half), F32),) * (2 * S5_PASS))
        lax.fori_loop(0, n_lat, lat_step, carry)

        for q in range(S5_PASS):
            g8 = first + q
            ent = jnp.concatenate([jnp.where(sel, fw_re[q], bw_re[q]),
                                   jnp.where(sel, fw_im[q], bw_im[q])], axis=1).astype(BF16)
            strip = m_ref[g8]
            for sig in range(S5_CHUNK):
                lo = (S5_CHUNK - 1 - sig) * S5_GROUP
                m_scr[sig * S5_GROUP:(sig + 1) * S5_GROUP, :] = strip[:, lo:lo + S5_CHUNK * S5_GROUP].astype(BF16)
            y = (jnp.dot(u_scr[q], m_scr[...], preferred_element_type=F32)
                 + jnp.dot(ent, wc_ref[g8], preferred_element_type=F32))
            y = _gelu_tanh(y)
            y_a[pl.ds(g8, r_lat, stride=S5_GB), :] = y[:, :128]
            y_b[pl.ds(g8, r_lat, stride=S5_GB), :] = y[:, 128:]
        return 0

    lax.fori_loop(0, S5_GB // S5_PASS, one_pass, 0)

    cpb = math.gcd(n_lat, 16)

    def relayout(i, _):
        src = pl.ds(pl.multiple_of(i * (cpb * bsz * S5_GB), cpb * bsz * S5_GB), cpb * bsz * S5_GB)
        dst = pl.ds(pl.multiple_of(i * (cpb * S5_CHUNK), cpb * S5_CHUNK), cpb * S5_CHUNK)
        t_a = _gran_transpose(y_a[src, :].reshape(cpb * bsz, 8, 128)).reshape(cpb, bsz, 8, 128)
        t_b = _gran_transpose(y_b[src, :].reshape(cpb * bsz, 8, 128)).reshape(cpb, bsz, 8, 128)
        for b in range(bsz):
            tok = jnp.concatenate([t_a[:, b][:, None], t_b[:, b][:, None]], axis=1)
            o_ref[b, dst, :] = tok.reshape(cpb * S5_CHUNK, 128).astype(o_ref.dtype)
        return 0

    lax.fori_loop(0, n_lat // cpb, relayout, 0)


def _s5(ua, ub, uca, ucb, wb, wc, m, ar, ai, n_ctx, n_lat, bsz):
    ngb = ua.shape[0]
    half = 2 * S5_STATE
    w = S5_CHUNK * S5_GROUP
    r_ctx, r_lat = n_ctx * bsz, n_lat * bsz
    u_blk = lambda r: pl.BlockSpec((1, r * S5_GB, 128), lambda i: (i, 0, 0), pipeline_mode=pl.Buffered(1))
    w_blk = lambda shape: pl.BlockSpec((S5_GB,) + shape, lambda i: (i, 0, 0))
    pass_scr = lambda r: pltpu.VMEM((S5_PASS, r, half), F32)
    return pl.pallas_call(
        functools.partial(_s5_kernel, n_ctx=n_ctx, n_lat=n_lat, bsz=bsz),
        grid=(ngb,),
        in_specs=[u_blk(r_lat), u_blk(r_lat), u_blk(r_ctx), u_blk(r_ctx),
                  w_blk((w, 2 * half)), w_blk((2 * half, w)), w_blk(m.shape[1:]), w_blk((1, half)), w_blk((1, half))],
        out_specs=pl.BlockSpec((bsz, n_lat * S5_CHUNK, 128), lambda i: (0, 0, i)),
        out_shape=jax.ShapeDtypeStruct((bsz, n_lat * S5_CHUNK, ngb * 128), BF16),
        scratch_shapes=[pass_scr(r_ctx + r_lat)] * 2 + [pass_scr(r_lat)] * 4
                       + [pltpu.VMEM((r_lat * S5_GB, 128), F32)] * 2
                       + [pltpu.VMEM((w, w), BF16), pltpu.VMEM((S5_PASS, r_lat, w), BF16)],
        compiler_params=_params(("parallel",)),
        name="s5",
    )(ua, ub, uca, ucb, wb, wc, m, ar, ai)


def _split_bf16(v):
    hi = v.astype(BF16)
    return hi, (v - hi.astype(F32)).astype(BF16)


def _filt_kernel(z_ref, w1_ref, b1_ref, w2_ref, b2_ref, w3b_ref, w3f_ref, fr_ref, decb_ref, decf_ref, o_ref,
                 hi_scr, lo_scr):
    n = z_ref.shape[0]
    e = z_ref.shape[1] // 2
    f = hi_scr.shape[1] // 2

    @pl.when((pl.program_id(0) == 0) & (pl.program_id(1) == 0))
    def _():
        fr = fr_ref[...]
        h = jnp.sin(fr[0:1] * (jnp.dot(z_ref[...], w1_ref[...], precision=HIGHEST, preferred_element_type=F32)
                               + b1_ref[...]))
        h = jnp.sin(fr[1:2] * (jnp.dot(h, w2_ref[...], precision=HIGHEST, preferred_element_type=F32)
                               + b2_ref[...]))
        hi_scr[...], lo_scr[...] = _split_bf16(h)

    def last_layer(cols, w_ref):
        w_hi, w_lo = _split_bf16(w_ref[...])
        h_hi = hi_scr[:, cols]
        return (jnp.dot(h_hi, w_hi, preferred_element_type=F32) + jnp.dot(h_hi, w_lo, preferred_element_type=F32)
                + jnp.dot(lo_scr[:, cols], w_hi, preferred_element_type=F32))

    bwd = last_layer(slice(0, f), w3b_ref) * jnp.exp(-z_ref[:, 0:1] * jnp.abs(decb_ref[...]))
    bwd = jnp.where(lax.broadcasted_iota(jnp.int32, bwd.shape, 0) == 0, 0.0, bwd)
    fwd = last_layer(slice(f, 2 * f), w3f_ref) * jnp.exp(-z_ref[:, e:e + 1] * jnp.abs(decf_ref[...]))
    den = jnp.sum(jnp.abs(bwd), axis=0, keepdims=True) + jnp.sum(jnp.abs(fwd), axis=0, keepdims=True) + EPS
    o_ref[0, 0:n, :] = bwd / den
    o_ref[0, n:, :] = fwd / den


def _filters(z2, w1, b1, w2, b2, w3, freq, decay, hy_w, tn=512):
    n2, e = z2.shape
    n = n2 // 2
    f = w1.shape[1]
    nt = hy_w // tn
    zz = jnp.concatenate([z2[:n], z2[n:]], axis=1)
    twice = lambda w: jnp.concatenate([jnp.pad(w, ((0, 0), (0, w.shape[1]))), jnp.pad(w, ((0, 0), (w.shape[1], 0)))])
    pair = lambda v: jnp.concatenate([v, v], axis=-1)
    full = lambda shape: pl.BlockSpec(shape, lambda o, j: (0, 0))
    fwd_col = lambda r: pl.BlockSpec((r, tn), lambda o, j: (0, o * 2 * nt + j))
    bwd_col = lambda r: pl.BlockSpec((r, tn), lambda o, j: (0, o * 2 * nt + nt + j))
    dec = decay.reshape(1, -1)
    return pl.pallas_call(
        _filt_kernel,
        grid=(HY_ORDER, nt),
        in_specs=[full((n, 2 * e)), full((2 * e, 2 * f)), full((1, 2 * f)), full((2 * f, 2 * f)), full((1, 2 * f)),
                  bwd_col(f), fwd_col(f), full((2, 2 * f)), bwd_col(1), fwd_col(1)],
        out_specs=pl.BlockSpec((1, n2, tn), lambda o, j: (o, 0, j)),
        out_shape=jax.ShapeDtypeStruct((HY_ORDER, n2, hy_w), F32),
        scratch_shapes=[pltpu.VMEM((n, 2 * f), BF16)] * 2,
        compiler_params=_params(("arbitrary", "arbitrary")),
        name="filt",
    )(zz, twice(w1), pair(b1.reshape(1, f)), twice(w2), pair(b2.reshape(1, f)), w3, w3, pair(freq), dec, dec)


HY_BLOCKS = 4


def _dft_tables(n):
    k = np.arange(n, dtype=np.int64)
    ang = (np.outer(k, k) % (2 * n)).astype(np.float64) * (math.pi / n)
    return jnp.asarray(np.cos(ang), dtype=F32).astype(BF16), jnp.asarray(-np.sin(ang), dtype=F32).astype(BF16)


def _alt(n):
    return jnp.where(lax.broadcasted_iota(jnp.int32, (n, 1), 0) % 2 == 0, 1.0, -1.0).astype(F32)


def _spec_kernel(h_ref, c_ref, s_ref, hr_ref, hi_ref, hn_ref):
    m = c_ref.shape[0]
    cmat = c_ref[...]
    smat = s_ref[...]
    alt = _alt(m)
    wk = jnp.where(lax.broadcasted_iota(jnp.int32, (m, 1), 0) == 0, 0.5 / m, 1.0 / m)
    prev = None
    for j in range(2 * HY_BLOCKS):
        blk = h_ref[0, j * m:(j + 1) * m, :]
        bb = blk.astype(BF16)
        cur = (jnp.dot(cmat, bb, preferred_element_type=F32), jnp.dot(smat, bb, preferred_element_type=F32),
               jnp.sum(blk * alt, axis=0, keepdims=True), blk[0:1])
        if prev is not None:
            hr_ref[0, j - 1] = ((cur[0] + alt * (prev[0] - prev[3])) * wk).astype(hr_ref.dtype)
            hi_ref[0, j - 1] = ((cur[1] + alt * prev[1]) * wk).astype(hi_ref.dtype)
            hn_ref[0, j - 1] = (cur[2] + prev[2] - prev[3]) * (0.5 / m)
        prev = cur


def _spectrum(taps, cmat, smat, tn=256):
    _, n2, hy_w = taps.shape
    m = cmat.shape[0]
    nd = 2 * HY_BLOCKS - 1
    const = lambda shape: pl.BlockSpec(shape, lambda o, j: (0, 0), pipeline_mode=pl.Buffered(1))
    out_blk = lambda r: pl.BlockSpec((1, nd, r, tn), lambda o, j: (o, 0, 0, j))
    return pl.pallas_call(
        _spec_kernel,
        grid=(HY_ORDER, hy_w // tn),
        in_specs=[pl.BlockSpec((1, n2, tn), lambda o, j: (o, 0, j)), const((m, m)), const((m, m))],
        out_specs=[out_blk(m), out_blk(m), out_blk(1)],
        out_shape=[jax.ShapeDtypeStruct((HY_ORDER, nd, m, hy_w), BF16),
                   jax.ShapeDtypeStruct((HY_ORDER, nd, m, hy_w), BF16),
                   jax.ShapeDtypeStruct((HY_ORDER, nd, 1, hy_w), F32)],
        compiler_params=_params(("parallel", "parallel")),
        name="spec",
    )(taps, cmat, smat)


def _hconv_kernel(z_ref, gate_ref, bias_ref, hr_ref, hi_ref, hn_ref, c_ref, s_ref, o_ref):
    m = c_ref.shape[0]
    cmat = c_ref[...]
    smat = s_ref[...]
    alt = _alt(m)
    zr, zi, zn = [], [], []
    for j in range(HY_BLOCKS):
        zj = z_ref[0, j * m:(j + 1) * m, :]
        zr.append(jnp.dot(cmat, zj, preferred_element_type=F32).astype(BF16))
        zi.append(jnp.dot(smat, zj, preferred_element_type=F32).astype(BF16))
        zn.append(jnp.sum(zj.astype(F32) * alt, axis=0, keepdims=True))
    for i in range(HY_BLOCKS):
        yr = yi = yn = None
        for j in range(HY_BLOCKS):
            d = i - j + HY_BLOCKS - 1
            hr = hr_ref[0, d]
            hi = hi_ref[0, d]
            pr = zr[j] * hr - zi[j] * hi
            pi = zr[j] * hi + zi[j] * hr
            pn = zn[j] * hn_ref[0, d]
            yr, yi, yn = (pr, pi, pn) if yr is None else (yr + pr, yi + pi, yn + pn)
        rows = slice(i * m, (i + 1) * m)
        y = (jnp.dot(cmat, yr, preferred_element_type=F32) + jnp.dot(smat, yi, preferred_element_type=F32)
             + alt * yn + bias_ref[0] * z_ref[0, rows, :].astype(F32))
        o_ref[0, rows, :] = (gate_ref[0, rows, :].astype(F32) * y).astype(o_ref.dtype)


def _hconv(zsrc, z_off, gsrc, g_off, bias, hr, hi, hn, order, cmat, smat, hy_w, tc=256):
    bsz, n, _ = zsrc.shape
    m = cmat.shape[0]
    nd = 2 * HY_BLOCKS - 1
    nt = hy_w // tc
    zo, go = z_off // tc, g_off // tc
    const = lambda shape: pl.BlockSpec(shape, lambda j, b: (0, 0), pipeline_mode=pl.Buffered(1))
    spec_blk = lambda r: pl.BlockSpec((1, nd, r, tc), lambda j, b: (order, 0, 0, j))
    return pl.pallas_call(
        _hconv_kernel,
        grid=(nt, bsz),
        in_specs=[pl.BlockSpec((1, n, tc), lambda j, b: (b, 0, zo + j)),
                  pl.BlockSpec((1, n, tc), lambda j, b: (b, 0, go + j)),
                  pl.BlockSpec((1, 1, tc), lambda j, b: (order, 0, j)),
                  spec_blk(m), spec_blk(m), spec_blk(1),
                  const((m, m)), const((m, m))],
        out_specs=pl.BlockSpec((1, n, tc), lambda j, b: (b, 0, j)),
        out_shape=jax.ShapeDtypeStruct((bsz, n, hy_w), BF16),
        compiler_params=_params(("parallel", "parallel")),
        name="hconv",
    )(zsrc, gsrc, bias, hr, hi, hn, cmat, smat)


def _mix_kernel(ys_ref, yh_ref, x_ref, gw_ref, gb_ref, gs_ref, gh_ref, wo_ref, g1_ref, o_ref, *, s5_w):
    gl = jnp.dot(ys_ref[0], gw_ref[...], preferred_element_type=F32) + gb_ref[...]
    a = gl[:, :s5_w] * jax.nn.sigmoid(gl[:, s5_w:])
    a = _rms(a, gs_ref[...]).astype(BF16)
    yh = _rms(yh_ref[0].astype(F32), gh_ref[...]).astype(BF16)
    proj = (jnp.dot(a, wo_ref[:s5_w, :], preferred_element_type=F32)
            + jnp.dot(yh, wo_ref[s5_w:, :], preferred_element_type=F32))
    o_ref[0] = x_ref[0] + g1_ref[0] * proj


def _mix(ys, yh, x, glu_w, glu_b, g_s5, g_hy, w_out, g1, tm=512):
    bsz, n, d = x.shape
    tm = min(tm, n)
    s5_w = ys.shape[-1]
    hy_w = yh.shape[-1]
    const = lambda shape: pl.BlockSpec(shape, lambda b, i: (0, 0), pipeline_mode=pl.Buffered(1))
    tok = lambda w: pl.BlockSpec((1, tm, w), lambda b, i: (b, i, 0))
    return pl.pallas_call(
        functools.partial(_mix_kernel, s5_w=s5_w),
        grid=(bsz, n // tm),
        in_specs=[tok(s5_w), tok(hy_w), tok(d),
                  const((s5_w, 2 * s5_w)), const((1, 2 * s5_w)), const((1, s5_w)), const((1, hy_w)),
                  const((s5_w + hy_w, d)),
                  pl.BlockSpec((1, 1, d), lambda b, i: (b, 0, 0))],
        out_specs=tok(d),
        out_shape=jax.ShapeDtypeStruct((bsz, n, d), F32),
        compiler_params=_params(("parallel", "parallel")),
        name="mix",
    )(ys, yh, x, glu_w, glu_b.reshape(1, -1), g_s5.reshape(1, -1), g_hy.reshape(1, -1), w_out, g1)


def _ffn_kernel(x_ref, g_ref, sh_ref, sc_ref, wg_ref, wu_ref, wd_ref, g2_ref, fg_ref, o_ref, h_scr, acc_scr):
    j = pl.program_id(2)

    @pl.when(j == 0)
    def _():
        _norm_modulate_rows(x_ref, g_ref, sh_ref, sc_ref, h_scr)
        acc_scr[...] = jnp.zeros_like(acc_scr)

    h = h_scr[...]
    gate = jnp.dot(h, wg_ref[...], preferred_element_type=F32)
    up = jnp.dot(h, wu_ref[...], preferred_element_type=F32)
    act = (gate * jax.nn.sigmoid(gate) * up).astype(BF16)
    acc_scr[...] += jnp.dot(act, wd_ref[...], preferred_element_type=F32)

    @pl.when(j == pl.num_programs(2) - 1)
    def _():
        gate2 = g2_ref[0]
        final_gain = fg_ref[...]

        def block(rows):
            o_ref[0, rows, :] = _rms(x_ref[0, rows, :] + gate2 * acc_scr[rows, :], final_gain)

        _by_rows(acc_scr.shape[0], block)


def _ffn(x, g, sh, sc, wg, wu, wd, g2, fg, tm=512, tf=512):
    bsz, n, d = x.shape
    tm = min(tm, n)
    dff = wg.shape[1]
    row = lambda: pl.BlockSpec((1, d), lambda b, i, j: (0, 0))
    per_b = lambda: pl.BlockSpec((1, 1, d), lambda b, i, j: (b, 0, 0))
    return pl.pallas_call(
        _ffn_kernel,
        grid=(bsz, n // tm, dff // tf),
        in_specs=[pl.BlockSpec((1, tm, d), lambda b, i, j: (b, i, 0)),
                  row(), per_b(), per_b(),
                  pl.BlockSpec((d, tf), lambda b, i, j: (0, j)),
                  pl.BlockSpec((d, tf), lambda b, i, j: (0, j)),
                  pl.BlockSpec((tf, d), lambda b, i, j: (j, 0)),
                  per_b(), row()],
        out_specs=pl.BlockSpec((1, tm, d), lambda b, i, j: (b, i, 0)),
        out_shape=jax.ShapeDtypeStruct((bsz, n, d), F32),
        scratch_shapes=[pltpu.VMEM((tm, d), BF16), pltpu.VMEM((tm, d), F32)],
        compiler_params=_params(("parallel", "parallel", "arbitrary")),
        name="ffn",
    )(x, g.reshape(1, d), sh, sc, wg, wu, wd, g2, fg.reshape(1, d))


def _positional_features(n):
    pos = jnp.abs(jnp.arange(2 * n, dtype=F32) - n)
    t = pos[:, None] / n
    bands = jnp.linspace(1e-4, HY_BANDS - 1, HY_BANDS, dtype=F32)
    ang = 2.0 * math.pi * pos[:, None] * bands[None, :] / n
    return jnp.concatenate([t, jnp.cos(ang), -jnp.sin(ang)], axis=-1)


def kernel(x, c, ctx, c_ctx, ada_w, ada_b, norm1_g, w_in, conv_w, conv_b, hy_w1, hy_b1, hy_w2, hy_b2, hy_w3,
           hy_sin_freq, hy_decay, hy_bias, s5_lam_re, s5_lam_im, s5_log_dt, s5_b_re, s5_b_im, s5_c_re, s5_c_im,
           s5_d, s5_glu_w, s5_glu_b, branch_g_s5, branch_g_hy, w_out, norm2_g, ffn_w_gate, ffn_w_up,
           ffn_w_down, final_g):
    bsz, n_lat, d = x.shape
    n_ctx = ctx.shape[1]
    assert ada_w.shape[0] == 1, "single-layer block"
    l = 0
    s5_w = s5_glu_w.shape[1]
    hy_w = w_in.shape[2] - s5_w
    hy_w //= 3
    groups = s5_w // S5_GROUP
    assert n_lat % S5_CHUNK == 0 and n_ctx % S5_CHUNK == 0 and n_lat % GRID_W == 0

    pad_rows = -(bsz + 1) % 8
    cc = jnp.concatenate([c, c_ctx[None, :], jnp.zeros((pad_rows, d), F32)], axis=0)
    mod = _ada(cc, ada_w[l], ada_b[l])
    sh1, sc1, g1, sh2, sc2, g2 = [mod[:bsz, None, i * d:(i + 1) * d] for i in range(6)]
    csh1, csc1 = [jnp.broadcast_to(mod[bsz, None, None, i * d:(i + 1) * d], (bsz, 1, d)) for i in range(2)]

    w_in_b = w_in[l].astype(BF16)
    ua, ub, p = _inproj(x, norm1_g[l], sh1, sc1, w_in_b, conv_w[l], conv_b[l], s5_w, tm=min(512, n_lat))
    uca, ucb = _inproj(ctx, norm1_g[l], csh1, csc1, w_in_b[:, :s5_w], None, None, s5_w, tm=min(256, n_ctx))

    wb, wc, m, ar, ai = _s5_prep(s5_lam_re[l], s5_lam_im[l], s5_log_dt[l], s5_b_re[l], s5_b_im[l],
                                 s5_c_re[l], s5_c_im[l], s5_d[l])
    flat = lambda u: u.reshape(u.shape[0], -1, 128)
    ys = _s5(flat(ua), flat(ub), flat(uca), flat(ucb), wb, wc, m, ar, ai,
             n_ctx // S5_CHUNK, n_lat // S5_CHUNK, bsz)

    z = _positional_features(n_lat)
    e_pad = -z.shape[1] % 128
    z = jnp.pad(z, ((0, 0), (0, e_pad)))
    w1 = jnp.pad(hy_w1[l], ((0, e_pad), (0, 0)))
    taps = _filters(z, w1, hy_b1[l], hy_w2[l], hy_b2[l], hy_w3[l], hy_sin_freq[l],
                    hy_decay[l].reshape(-1), hy_w)
    assert n_lat % (HY_BLOCKS * 128) == 0
    cmat, smat = _dft_tables(n_lat // HY_BLOCKS)
    hr, hi, hn = _spectrum(taps, cmat, smat)
    bias = hy_bias[l].reshape(HY_ORDER, 1, hy_w)
    z1 = _hconv(p, 0, p, hy_w, bias, hr, hi, hn, 0, cmat, smat, hy_w)
    yh = _hconv(z1, 0, p, 2 * hy_w, bias, hr, hi, hn, 1, cmat, smat, hy_w)

    x1 = _mix(ys, yh, x, s5_glu_w[l].astype(BF16), s5_glu_b[l], branch_g_s5[l], branch_g_hy[l],
              w_out[l].astype(BF16), g1)

    return _ffn(x1, norm2_g[l], sh2, sc2, ffn_w_gate[l].astype(BF16), ffn_w_up[l].astype(BF16),
                ffn_w_down[l].astype(BF16), g2, final_g)
```

```python
import functools
import math

import numpy as np
import jax
import jax.numpy as jnp
from jax import lax
from jax.experimental import pallas as pl
from jax.experimental.pallas import tpu as pltpu

EPS = 1e-6
GRID_W = 64
S5_GROUP = 16
S5_STATE = 64
S5_CHUNK = 16
HY_ORDER = 2
HY_BANDS = 16
V7X_VMEM_BYTES = 64 * 1024 * 1024
VMEM_LIMIT = 56 * 1024 * 1024

F32 = jnp.float32
BF16 = jnp.bfloat16
HIGHEST = lax.Precision.HIGHEST


def _params(sem):
    return pltpu.CompilerParams(dimension_semantics=sem, vmem_limit_bytes=VMEM_LIMIT)


def _rms(x, g):
    return x * lax.rsqrt(jnp.mean(x * x, axis=-1, keepdims=True) + EPS) * g


ROW_BLOCK = 16


def _by_rows(n_rows, fn):
    for r in range(0, n_rows, ROW_BLOCK):
        fn(slice(r, r + ROW_BLOCK))


def _norm_modulate_rows(x_ref, g_ref, sh_ref, sc_ref, h_ref):
    gain = g_ref[...] * (1.0 + sc_ref[0])
    shift = sh_ref[0]

    def block(rows):
        h_ref[rows, :] = (_rms(x_ref[0, rows, :], gain) + shift).astype(BF16)

    _by_rows(h_ref.shape[0], block)


def _split_bf16(v):
    hi = v.astype(BF16)
    return hi, (v - hi.astype(F32)).astype(BF16)


def _dot_3pass(a, b):
    a_hi, a_lo = _split_bf16(a)
    b_hi, b_lo = _split_bf16(b)
    return (jnp.dot(a_hi, b_hi, preferred_element_type=F32) + jnp.dot(a_hi, b_lo, preferred_element_type=F32)
            + jnp.dot(a_lo, b_hi, preferred_element_type=F32))


def _ada_kernel(c_ref, w_ref, b_ref, o_ref):
    cv = c_ref[...]
    o_ref[...] = _dot_3pass(cv * jax.nn.sigmoid(cv), w_ref[...]) + b_ref[...]


def _ada(cc, w, b, tn=2048):
    r, d = cc.shape
    n = w.shape[1]
    return pl.pallas_call(
        _ada_kernel,
        grid=(n // tn,),
        in_specs=[pl.BlockSpec((r, d), lambda j: (0, 0)),
                  pl.BlockSpec((d, tn), lambda j: (0, j)),
                  pl.BlockSpec((1, tn), lambda j: (0, j))],
        out_specs=pl.BlockSpec((r, tn), lambda j: (0, j)),
        out_shape=jax.ShapeDtypeStruct((r, n), F32),
        compiler_params=_params(("arbitrary",)),
        name="ada",
    )(cc, w, b.reshape(1, n))


def _gran_transpose(a):
    r = lax.broadcasted_iota(jnp.int32, a.shape, 1)
    q = lax.broadcasted_iota(jnp.int32, a.shape, 2) // S5_GROUP
    for dist in (4, 2, 1):
        rb = (r & dist) != 0
        qb = (q & dist) != 0
        up = pltpu.roll(pltpu.roll(a, 8 - dist, axis=1), S5_GROUP * dist, axis=2)
        dn = pltpu.roll(pltpu.roll(a, dist, axis=1), 128 - S5_GROUP * dist, axis=2)
        a = jnp.where(rb == qb, a, jnp.where(rb, dn, up))
    return a


def _inproj_kernel(*refs, s5_cols, tn, conv):
    if conv:
        x_ref, g_ref, sh_ref, sc_ref, w_ref, cw_ref, cb_ref, ua_ref, ub_ref, o_ref, h_scr = refs
    else:
        x_ref, g_ref, sh_ref, sc_ref, w_ref, ua_ref, ub_ref, h_scr = refs
    tm = x_ref.shape[1]
    _norm_modulate_rows(x_ref, g_ref, sh_ref, sc_ref, h_scr)

    for jn in range(s5_cols // tn):
        acc = jnp.dot(h_scr[...], w_ref[:, jn * tn:(jn + 1) * tn], preferred_element_type=F32)
        for cb in range(tn // 128):
            tile = acc[:, cb * 128:(cb + 1) * 128].reshape(tm // S5_CHUNK, 2, 8, 128)
            gb = jn * (tn // 128) + cb
            ua_ref[gb, :, 0] = _gran_transpose(tile[:, 0])
            ub_ref[gb, :, 0] = _gran_transpose(tile[:, 1])

    if conv:
        col = lax.broadcasted_iota(jnp.int32, (GRID_W, tn), 0)
        for jn in range((w_ref.shape[1] - s5_cols) // tn):
            c0 = jn * tn
            acc = jnp.dot(h_scr[...], w_ref[:, s5_cols + c0:s5_cols + c0 + tn], preferred_element_type=F32)
            cw = cw_ref[:, c0:c0 + tn]
            bias = cb_ref[:, c0:c0 + tn]
            for r0 in range(0, tm, GRID_W):
                row = acc[r0:r0 + GRID_W]
                prev = jnp.where(col == 0, 0.0, pltpu.roll(row, 1, axis=0))
                nxt = jnp.where(col == GRID_W - 1, 0.0, pltpu.roll(row, GRID_W - 1, axis=0))
                o_ref[0, r0:r0 + GRID_W, c0:c0 + tn] = (prev * cw[0:1] + row * cw[1:2] + nxt * cw[2:3]
                                                        + bias).astype(o_ref.dtype)


def _inproj(x, g, sh, sc, w, cw, cb, s5_cols, tm, tn=512):
    bsz, n, d = x.shape
    nn = w.shape[1]
    conv = cw is not None
    assert tm % GRID_W == 0 and n % tm == 0 and s5_cols % tn == 0 and (nn - s5_cols) % tn == 0
    ngb = s5_cols // 128
    const = lambda shape: pl.BlockSpec(shape, lambda b, i: (0,) * len(shape), pipeline_mode=pl.Buffered(1))
    per_b = pl.BlockSpec((1, 1, d), lambda b, i: (b, 0, 0))
    in_specs = [pl.BlockSpec((1, tm, d), lambda b, i: (b, i, 0)), const((1, d)), per_b, per_b, const((d, nn))]
    args = [x, g.reshape(1, d), sh, sc, w]
    u_spec = pl.BlockSpec((ngb, tm // S5_CHUNK, 1, 8, 128), lambda b, i: (0, i, b, 0, 0))
    u_shape = jax.ShapeDtypeStruct((ngb, n // S5_CHUNK, bsz, 8, 128), F32)
    out_specs, out_shape = [u_spec, u_spec], [u_shape, u_shape]
    if conv:
        in_specs += [const((3, nn - s5_cols)), const((1, nn - s5_cols))]
        args += [cw, cb.reshape(1, -1)]
        out_specs.append(pl.BlockSpec((1, tm, nn - s5_cols), lambda b, i: (b, i, 0)))
        out_shape.append(jax.ShapeDtypeStruct((bsz, n, nn - s5_cols), BF16))
    return pl.pallas_call(
        functools.partial(_inproj_kernel, s5_cols=s5_cols, tn=tn, conv=conv),
        grid=(bsz, n // tm),
        in_specs=in_specs,
        out_specs=out_specs,
        out_shape=out_shape,
        scratch_shapes=[pltpu.VMEM((tm, d), BF16)],
        compiler_params=_params(("parallel", "parallel")),
        name="inproj" if conv else "inproj_ctx",
    )(*args)


def _s5_prep(lam_re, lam_im, log_dt, b_re, b_im, c_re, c_im, d):
    t = S5_CHUNK
    lam_re, lam_im = lam_re.astype(F32), lam_im.astype(F32)
    dt = jnp.exp(log_dt.astype(F32))[..., None]
    k = jnp.arange(t + 1, dtype=F32)[:, None, None, None]
    mag = jnp.exp(lam_re * dt * k)
    pr, pi = mag * jnp.cos(lam_im * dt * k), mag * jnp.sin(lam_im * dt * k)
    er, ei = pr[1] - 1.0, pi[1]
    den = lam_re * lam_re + lam_im * lam_im
    qr, qi = (er * lam_re + ei * lam_im) / den, (ei * lam_re - er * lam_im) / den
    b_re, b_im = b_re.astype(F32), b_im.astype(F32)
    bbr = qr[..., None] * b_re - qi[..., None] * b_im
    bbi = qr[..., None] * b_im + qi[..., None] * b_re
    cr, ci = c_re.astype(F32), c_im.astype(F32)
    g = lam_re.shape[1]

    def times_b(powr, powi, dr):
        ar_, ai_ = powr.transpose(1, 0, 2)[:, :, None, :], powi.transpose(1, 0, 2)[:, :, None, :]
        br_, bi_ = bbr[dr].transpose(0, 2, 1)[:, None], bbi[dr].transpose(0, 2, 1)[:, None]
        return ar_ * br_ - ai_ * bi_, ar_ * bi_ + ai_ * br_

    wbf_r, wbf_i = times_b(pr[t - 1::-1, 0], pi[t - 1::-1, 0], 0)
    wbb_r, wbb_i = times_b(pr[:t, 1], pi[:t, 1], 1)
    wb = jnp.concatenate([wbf_r, wbb_r, wbf_i, wbb_i], axis=-1).reshape(g, t * S5_GROUP, 4 * S5_STATE)

    def c_times(powr, powi, dr):
        cr_, ci_ = cr[dr].transpose(0, 2, 1)[:, :, None, :], ci[dr].transpose(0, 2, 1)[:, :, None, :]
        ar_, ai_ = powr.transpose(1, 2, 0)[..., None], powi.transpose(1, 2, 0)[..., None]
        return cr_ * ar_ - ci_ * ai_, cr_ * ai_ + ci_ * ar_

    wcf_r, wcf_i = c_times(pr[1:, 0], pi[1:, 0], 0)
    wcb_r, wcb_i = c_times(pr[t:0:-1, 1], pi[t:0:-1, 1], 1)
    wc = jnp.concatenate([wcf_r, wcb_r, -wcf_i, -wcb_i], axis=1).reshape(g, 4 * S5_STATE, t * S5_GROUP)

    def lag_kernel(dr):
        abr = pr[:t, dr, :, :, None] * bbr[dr] - pi[:t, dr, :, :, None] * bbi[dr]
        abi = pr[:t, dr, :, :, None] * bbi[dr] + pi[:t, dr, :, :, None] * bbr[dr]
        return (jnp.einsum('kgpe,gcp->kgec', abr, cr[dr], precision=HIGHEST)
                - jnp.einsum('kgpe,gcp->kgec', abi, ci[dr], precision=HIGHEST))

    kf, kb = lag_kernel(0), lag_kernel(1)
    kf = kf.at[0].add(jnp.eye(S5_GROUP, dtype=F32) * d.astype(F32)[:, None, :])
    strip = jnp.concatenate([kb[:0:-1], (kf[0] + kb[0])[None], kf[1:]], axis=0)
    strip = strip.transpose(1, 2, 0, 3).reshape(g, S5_GROUP, (2 * t - 1) * S5_GROUP)
    m = jnp.pad(strip, ((0, 0), (0, 0), (0, S5_GROUP)))

    ar = jnp.concatenate([pr[t, 0], pr[t, 1]], axis=-1)[:, None, :]
    ai = jnp.concatenate([pi[t, 0], pi[t, 1]], axis=-1)[:, None, :]
    return wb.astype(BF16), wc.astype(BF16), m, ar, ai


def _gelu_tanh(x):
    return 0.5 * x * (1.0 + jnp.tanh(math.sqrt(2.0 / math.pi) * (x + 0.044715 * (x * x * x))))


S5_GB = 8
S5_PASS = 4


def _s5_kernel(ua_ref, ub_ref, uca_ref, ucb_ref, wb_ref, wc_ref, m_ref, ar_ref, ai_ref, o_ref,
               p_re, p_im, fw_re, fw_im, bw_re, bw_im, y_a, y_b, m_scr, u_scr, *, n_ctx, n_lat, bsz):
    half = 2 * S5_STATE
    r_ctx, r_lat = n_ctx * bsz, n_lat * bsz
    is_fwd = lax.broadcasted_iota(jnp.int32, (bsz, half), 1) < S5_STATE
    sel = lax.broadcasted_iota(jnp.int32, (r_lat, half), 1) < S5_STATE

    def load_u(a_ref, b_ref, g8, r):
        return jnp.concatenate([a_ref[0, pl.ds(g8, r, stride=S5_GB), :],
                                b_ref[0, pl.ds(g8, r, stride=S5_GB), :]], axis=1).astype(BF16)

    def rows(i):
        return pl.ds(pl.multiple_of(i * bsz, bsz), bsz)

    def one_pass(pass_idx, _):
        first = pass_idx * S5_PASS
        for q in range(S5_PASS):
            wbg = wb_ref[first + q]
            pc = jnp.dot(load_u(uca_ref, ucb_ref, first + q, r_ctx), wbg, preferred_element_type=F32)
            u_scr[q] = load_u(ua_ref, ub_ref, first + q, r_lat)
            pn = jnp.dot(u_scr[q], wbg, preferred_element_type=F32)
            p_re[q, 0:r_ctx, :] = pc[:, :half]
            p_im[q, 0:r_ctx, :] = pc[:, half:]
            p_re[q, r_ctx:, :] = pn[:, :half]
            p_im[q, r_ctx:, :] = pn[:, half:]

        ars = [jnp.broadcast_to(ar_ref[first + q], (bsz, half)) for q in range(S5_PASS)]
        ais = [jnp.broadcast_to(ai_ref[first + q], (bsz, half)) for q in range(S5_PASS)]

        def advance(carry, fi, bi):
            out = []
            for q in range(S5_PASS):
                s_re, s_im = carry[2 * q], carry[2 * q + 1]
                in_re = jnp.where(is_fwd, p_re[q, rows(fi), :], p_re[q, rows(bi), :])
                in_im = jnp.where(is_fwd, p_im[q, rows(fi), :], p_im[q, rows(bi), :])
                out += [ars[q] * s_re - ais[q] * s_im + in_re, ars[q] * s_im + ais[q] * s_re + in_im]
            return tuple(out)

        def ctx_step(k, carry):
            return advance(carry, k, n_ctx - 1 - k)

        def lat_step(k, carry):
            kb = n_lat - 1 - k
            for q in range(S5_PASS):
                fw_re[q, rows(k), :] = carry[2 * q]
                fw_im[q, rows(k), :] = carry[2 * q + 1]
                bw_re[q, rows(kb), :] = carry[2 * q]
                bw_im[q, rows(kb), :] = carry[2 * q + 1]
            return advance(carry, n_ctx + k, n_ctx + kb)

        carry = lax.fori_loop(0, n_ctx, ctx_step, (jnp.zeros((bsz, half), F32),) * (2 * S5_PASS))
        lax.fori_loop(0, n_lat, lat_step, carry)

        for q in range(S5_PASS):
            g8 = first + q
            ent = jnp.concatenate([jnp.where(sel, fw_re[q], bw_re[q]),
                                   jnp.where(sel, fw_im[q], bw_im[q])], axis=1).astype(BF16)
            strip = m_ref[g8]
            for sig in range(S5_CHUNK):
                lo = (S5_CHUNK - 1 - sig) * S5_GROUP
                m_scr[sig * S5_GROUP:(sig + 1) * S5_GROUP, :] = strip[:, lo:lo + S5_CHUNK * S5_GROUP].astype(BF16)
            y = (jnp.dot(u_scr[q], m_scr[...], preferred_element_type=F32)
                 + jnp.dot(ent, wc_ref[g8], preferred_element_type=F32))
            y = _gelu_tanh(y)
            y_a[pl.ds(g8, r_lat, stride=S5_GB), :] = y[:, :128]
            y_b[pl.ds(g8, r_lat, stride=S5_GB), :] = y[:, 128:]
        return 0

    lax.fori_loop(0, S5_GB // S5_PASS, one_pass, 0)

    cpb = math.gcd(n_lat, 16)

    def relayout(i, _):
        src = pl.ds(pl.multiple_of(i * (cpb * bsz * S5_GB), cpb * bsz * S5_GB), cpb * bsz * S5_GB)
        dst = pl.ds(pl.multiple_of(i * (cpb * S5_CHUNK), cpb * S5_CHUNK), cpb * S5_CHUNK)
        t_a = _gran_transpose(y_a[src, :].reshape(cpb * bsz, 8, 128)).reshape(cpb, bsz, 8, 128)
        t_b = _gran_transpose(y_b[src, :].reshape(cpb * bsz, 8, 128)).reshape(cpb, bsz, 8, 128)
        for b in range(bsz):
            tok = jnp.concatenate([t_a[:, b][:, None], t_b[:, b][:, None]], axis=1)
            o_ref[b, dst, :] = tok.reshape(cpb * S5_CHUNK, 128).astype(o_ref.dtype)
        return 0

    lax.fori_loop(0, n_lat // cpb, relayout, 0)


def _s5(ua, ub, uca, ucb, wb, wc, m, ar, ai, n_ctx, n_lat, bsz):
    ngb = ua.shape[0]
    half = 2 * S5_STATE
    w = S5_CHUNK * S5_GROUP
    r_ctx, r_lat = n_ctx * bsz, n_lat * bsz
    u_blk = lambda r: pl.BlockSpec((1, r * S5_GB, 128), lambda i: (i, 0, 0))
    w_blk = lambda shape: pl.BlockSpec((S5_GB,) + shape, lambda i: (i, 0, 0))
    pass_scr = lambda r: pltpu.VMEM((S5_PASS, r, half), F32)
    return pl.pallas_call(
        functools.partial(_s5_kernel, n_ctx=n_ctx, n_lat=n_lat, bsz=bsz),
        grid=(ngb,),
        in_specs=[u_blk(r_lat), u_blk(r_lat), u_blk(r_ctx), u_blk(r_ctx),
                  w_blk((w, 2 * half)), w_blk((2 * half, w)), w_blk(m.shape[1:]), w_blk((1, half)), w_blk((1, half))],
        out_specs=pl.BlockSpec((bsz, n_lat * S5_CHUNK, 128), lambda i: (0, 0, i)),
        out_shape=jax.ShapeDtypeStruct((bsz, n_lat * S5_CHUNK, ngb * 128), BF16),
        scratch_shapes=[pass_scr(r_ctx + r_lat)] * 2 + [pass_scr(r_lat)] * 4
                       + [pltpu.VMEM((r_lat * S5_GB, 128), F32)] * 2
                       + [pltpu.VMEM((w, w), BF16), pltpu.VMEM((S5_PASS, r_lat, w), BF16)],
        compiler_params=_params(("parallel",)),
        name="s5",
    )(ua, ub, uca, ucb, wb, wc, m, ar, ai)


def _filt_kernel(z_ref, w1_ref, b1_ref, w2_ref, b2_ref, w3b_ref, w3f_ref, fr_ref, decb_ref, decf_ref, o_ref,
                 hi_scr, lo_scr):
    n = z_ref.shape[0]
    e = z_ref.shape[1] // 2
    f = hi_scr.shape[1] // 2

    @pl.when((pl.program_id(0) == 0) & (pl.program_id(1) == 0))
    def _():
        fr = fr_ref[...]
        h = jnp.sin(fr[0:1] * (jnp.dot(z_ref[...], w1_ref[...], precision=HIGHEST, preferred_element_type=F32)
                               + b1_ref[...]))
        h = jnp.sin(fr[1:2] * (jnp.dot(h, w2_ref[...], precision=HIGHEST, preferred_element_type=F32)
                               + b2_ref[...]))
        hi_scr[...], lo_scr[...] = _split_bf16(h)

    def last_layer(cols, w_ref):
        w_hi, w_lo = _split_bf16(w_ref[...])
        h_hi = hi_scr[:, cols]
        return (jnp.dot(h_hi, w_hi, preferred_element_type=F32) + jnp.dot(h_hi, w_lo, preferred_element_type=F32)
                + jnp.dot(lo_scr[:, cols], w_hi, preferred_element_type=F32))

    bwd = last_layer(slice(0, f), w3b_ref) * jnp.exp(-z_ref[:, 0:1] * jnp.abs(decb_ref[...]))
    bwd = jnp.where(lax.broadcasted_iota(jnp.int32, bwd.shape, 0) == 0, 0.0, bwd)
    fwd = last_layer(slice(f, 2 * f), w3f_ref) * jnp.exp(-z_ref[:, e:e + 1] * jnp.abs(decf_ref[...]))
    den = jnp.sum(jnp.abs(bwd), axis=0, keepdims=True) + jnp.sum(jnp.abs(fwd), axis=0, keepdims=True) + EPS
    o_ref[0, 0:n, :] = bwd / den
    o_ref[0, n:, :] = fwd / den


def _filters(z2, w1, b1, w2, b2, w3, freq, decay, hy_w, tn=512):
    n2, e = z2.shape
    n = n2 // 2
    f = w1.shape[1]
    nt = hy_w // tn
    zz = jnp.concatenate([z2[:n], z2[n:]], axis=1)
    twice = lambda w: jnp.concatenate([jnp.pad(w, ((0, 0), (0, w.shape[1]))), jnp.pad(w, ((0, 0), (w.shape[1], 0)))])
    pair = lambda v: jnp.concatenate([v, v], axis=-1)
    full = lambda shape: pl.BlockSpec(shape, lambda o, j: (0, 0))
    fwd_col = lambda r: pl.BlockSpec((r, tn), lambda o, j: (0, o * 2 * nt + j))
    bwd_col = lambda r: pl.BlockSpec((r, tn), lambda o, j: (0, o * 2 * nt + nt + j))
    dec = decay.reshape(1, -1)
    return pl.pallas_call(
        _filt_kernel,
        grid=(HY_ORDER, nt),
        in_specs=[full((n, 2 * e)), full((2 * e, 2 * f)), full((1, 2 * f)), full((2 * f, 2 * f)), full((1, 2 * f)),
                  bwd_col(f), fwd_col(f), full((2, 2 * f)), bwd_col(1), fwd_col(1)],
        out_specs=pl.BlockSpec((1, n2, tn), lambda o, j: (o, 0, j)),
        out_shape=jax.ShapeDtypeStruct((HY_ORDER, n2, hy_w), F32),
        scratch_shapes=[pltpu.VMEM((n, 2 * f), BF16)] * 2,
        compiler_params=_params(("arbitrary", "arbitrary")),
        name="filt",
    )(zz, twice(w1), pair(b1.reshape(1, f)), twice(w2), pair(b2.reshape(1, f)), w3, w3, pair(freq), dec, dec)


HY_BLOCKS = 4


def _dft_tables(n):
    k = np.arange(n, dtype=np.int64)
    ang = (np.outer(k, k) % (2 * n)).astype(np.float64) * (math.pi / n)
    return jnp.asarray(np.cos(ang), dtype=F32).astype(BF16), jnp.asarray(-np.sin(ang), dtype=F32).astype(BF16)


def _alt(n):
    return jnp.where(lax.broadcasted_iota(jnp.int32, (n, 1), 0) % 2 == 0, 1.0, -1.0).astype(F32)


def _spec_kernel(h_ref, c_ref, s_ref, hr_ref, hi_ref, hn_ref):
    m = c_ref.shape[0]
    cmat = c_ref[...]
    smat = s_ref[...]
    alt = _alt(m)
    wk = jnp.where(lax.broadcasted_iota(jnp.int32, (m, 1), 0) == 0, 0.5 / m, 1.0 / m)
    prev = None
    for j in range(2 * HY_BLOCKS):
        blk = h_ref[0, j * m:(j + 1) * m, :]
        bb = blk.astype(BF16)
        cur = (jnp.dot(cmat, bb, preferred_element_type=F32), jnp.dot(smat, bb, preferred_element_type=F32),
               jnp.sum(blk * alt, axis=0, keepdims=True), blk[0:1])
        if prev is not None:
            hr_ref[0, j - 1] = ((cur[0] + alt * (prev[0] - prev[3])) * wk).astype(hr_ref.dtype)
            hi_ref[0, j - 1] = ((cur[1] + alt * prev[1]) * wk).astype(hi_ref.dtype)
            hn_ref[0, j - 1] = (cur[2] + prev[2] - prev[3]) * (0.5 / m)
        prev = cur


def _spectrum(taps, cmat, smat, tn=256):
    _, n2, hy_w = taps.shape
    m = cmat.shape[0]
    nd = 2 * HY_BLOCKS - 1
    const = lambda shape: pl.BlockSpec(shape, lambda o, j: (0, 0), pipeline_mode=pl.Buffered(1))
    out_blk = lambda r: pl.BlockSpec((1, nd, r, tn), lambda o, j: (o, 0, 0, j))
    return pl.pallas_call(
        _spec_kernel,
        grid=(HY_ORDER, hy_w // tn),
        in_specs=[pl.BlockSpec((1, n2, tn), lambda o, j: (o, 0, j)), const((m, m)), const((m, m))],
        out_specs=[out_blk(m), out_blk(m), out_blk(1)],
        out_shape=[jax.ShapeDtypeStruct((HY_ORDER, nd, m, hy_w), BF16),
                   jax.ShapeDtypeStruct((HY_ORDER, nd, m, hy_w), BF16),
                   jax.ShapeDtypeStruct((HY_ORDER, nd, 1, hy_w), F32)],
        compiler_params=_params(("parallel", "parallel")),
        name="spec",
    )(taps, cmat, smat)


def _hconv_kernel(v_ref, x1_ref, x2_ref, bias_ref, hr_ref, hi_ref, hn_ref, c_ref, s_ref, o_ref, z1_scr):
    m = c_ref.shape[0]
    cmat = c_ref[...]
    smat = s_ref[...]
    alt = _alt(m)

    def gated_conv(order, read_z, read_gate, write):
        zr, zi, zn = [], [], []
        for j in range(HY_BLOCKS):
            zj = read_z(slice(j * m, (j + 1) * m))
            zr.append(jnp.dot(cmat, zj, preferred_element_type=F32).astype(BF16))
            zi.append(jnp.dot(smat, zj, preferred_element_type=F32).astype(BF16))
            zn.append(jnp.sum(zj.astype(F32) * alt, axis=0, keepdims=True))
        for i in range(HY_BLOCKS):
            yr = yi = yn = None
            for j in range(HY_BLOCKS):
                d = i - j + HY_BLOCKS - 1
                hr = hr_ref[order, d]
                hi = hi_ref[order, d]
                pr = zr[j] * hr - zi[j] * hi
                pi = zr[j] * hi + zi[j] * hr
                pn = zn[j] * hn_ref[order, d]
                yr, yi, yn = (pr, pi, pn) if yr is None else (yr + pr, yi + pi, yn + pn)
            rows = slice(i * m, (i + 1) * m)
            y = (jnp.dot(cmat, yr, preferred_element_type=F32) + jnp.dot(smat, yi, preferred_element_type=F32)
                 + alt * yn + bias_ref[order] * read_z(rows).astype(F32))
            write(rows, (read_gate(rows).astype(F32) * y).astype(BF16))

    def write_z1(rows, val):
        z1_scr[rows, :] = val

    def write_out(rows, val):
        o_ref[0, rows, :] = val

    gated_conv(0, lambda r: v_ref[0, r, :], lambda r: x1_ref[0, r, :], write_z1)
    gated_conv(1, lambda r: z1_scr[r, :], lambda r: x2_ref[0, r, :], write_out)


def _hconv(hs, bias, hr, hi, hn, cmat, smat, hy_w, tc=256):
    assert HY_ORDER == 2
    bsz, n, _ = hs.shape
    m = cmat.shape[0]
    nd = 2 * HY_BLOCKS - 1
    nt = hy_w // tc
    const = lambda shape: pl.BlockSpec(shape, lambda j, b: (0, 0), pipeline_mode=pl.Buffered(1))
    spec_blk = lambda r: pl.BlockSpec((HY_ORDER, nd, r, tc), lambda j, b: (0, 0, 0, j))
    window = lambda k: pl.BlockSpec((1, n, tc), lambda j, b: (b, 0, k * nt + j))
    return pl.pallas_call(
        _hconv_kernel,
        grid=(nt, bsz),
        in_specs=[window(0), window(1), window(2),
                  pl.BlockSpec((HY_ORDER, 1, tc), lambda j, b: (0, 0, j)),
                  spec_blk(m), spec_blk(m), spec_blk(1),
                  const((m, m)), const((m, m))],
        out_specs=pl.BlockSpec((1, n, tc), lambda j, b: (b, 0, j)),
        out_shape=jax.ShapeDtypeStruct((bsz, n, hy_w), BF16),
        scratch_shapes=[pltpu.VMEM((n, tc), BF16)],
        compiler_params=_params(("parallel", "parallel")),
        name="hconv",
    )(hs, hs, hs, bias, hr, hi, hn, cmat, smat)


def _mix_kernel(ys_ref, yh_ref, x_ref, gw_ref, gb_ref, gs_ref, gh_ref, wo_ref, g1_ref, o_ref, *, s5_w):
    gl = jnp.dot(ys_ref[0], gw_ref[...], preferred_element_type=F32) + gb_ref[...]
    a = gl[:, :s5_w] * jax.nn.sigmoid(gl[:, s5_w:])
    a = _rms(a, gs_ref[...]).astype(BF16)
    yh = _rms(yh_ref[0].astype(F32), gh_ref[...]).astype(BF16)
    proj = (jnp.dot(a, wo_ref[:s5_w, :], preferred_element_type=F32)
            + jnp.dot(yh, wo_ref[s5_w:, :], preferred_element_type=F32))
    o_ref[0] = x_ref[0] + g1_ref[0] * proj


def _mix(ys, yh, x, glu_w, glu_b, g_s5, g_hy, w_out, g1, tm=512):
    bsz, n, d = x.shape
    tm = min(tm, n)
    s5_w = ys.shape[-1]
    hy_w = yh.shape[-1]
    const = lambda shape: pl.BlockSpec(shape, lambda b, i: (0, 0), pipeline_mode=pl.Buffered(1))
    tok = lambda w: pl.BlockSpec((1, tm, w), lambda b, i: (b, i, 0))
    return pl.pallas_call(
        functools.partial(_mix_kernel, s5_w=s5_w),
        grid=(bsz, n // tm),
        in_specs=[tok(s5_w), tok(hy_w), tok(d),
                  const((s5_w, 2 * s5_w)), const((1, 2 * s5_w)), const((1, s5_w)), const((1, hy_w)),
                  const((s5_w + hy_w, d)),
                  pl.BlockSpec((1, 1, d), lambda b, i: (b, 0, 0))],
        out_specs=tok(d),
        out_shape=jax.ShapeDtypeStruct((bsz, n, d), F32),
        compiler_params=_params(("parallel", "parallel")),
        name="mix",
    )(ys, yh, x, glu_w, glu_b.reshape(1, -1), g_s5.reshape(1, -1), g_hy.reshape(1, -1), w_out, g1)


def _ffn_kernel(x_ref, g_ref, sh_ref, sc_ref, wg_ref, wu_ref, wd_ref, g2_ref, fg_ref, o_ref, h_scr, acc_scr):
    j = pl.program_id(2)

    @pl.when(j == 0)
    def _():
        _norm_modulate_rows(x_ref, g_ref, sh_ref, sc_ref, h_scr)
        acc_scr[...] = jnp.zeros_like(acc_scr)

    h = h_scr[...]
    half = wg_ref.shape[1] // 2
    down = None
    for c0 in (0, half):
        gate = jnp.dot(h, wg_ref[:, c0:c0 + half], preferred_element_type=F32)
        up = jnp.dot(h, wu_ref[:, c0:c0 + half], preferred_element_type=F32)
        act = (gate * jax.nn.sigmoid(gate) * up).astype(BF16)
        part = jnp.dot(act, wd_ref[c0:c0 + half, :], preferred_element_type=F32)
        down = part if down is None else down + part
    acc_scr[...] += down

    @pl.when(j == pl.num_programs(2) - 1)
    def _():
        gate2 = g2_ref[0]
        final_gain = fg_ref[...]

        def block(rows):
            o_ref[0, rows, :] = _rms(x_ref[0, rows, :] + gate2 * acc_scr[rows, :], final_gain)

        _by_rows(acc_scr.shape[0], block)


def _ffn(x, g, sh, sc, wg, wu, wd, g2, fg, tm=512, tf=512):
    bsz, n, d = x.shape
    tm = min(tm, n)
    dff = wg.shape[1]
    row = lambda: pl.BlockSpec((1, d), lambda b, i, j: (0, 0))
    per_b = lambda: pl.BlockSpec((1, 1, d), lambda b, i, j: (b, 0, 0))
    return pl.pallas_call(
        _ffn_kernel,
        grid=(bsz, n // tm, dff // tf),
        in_specs=[pl.BlockSpec((1, tm, d), lambda b, i, j: (b, i, 0)),
                  row(), per_b(), per_b(),
                  pl.BlockSpec((d, tf), lambda b, i, j: (0, j)),
                  pl.BlockSpec((d, tf), lambda b, i, j: (0, j)),
                  pl.BlockSpec((tf, d), lambda b, i, j: (j, 0)),
                  per_b(), row()],
        out_specs=pl.BlockSpec((1, tm, d), lambda b, i, j: (b, i, 0)),
        out_shape=jax.ShapeDtypeStruct((bsz, n, d), F32),
        scratch_shapes=[pltpu.VMEM((tm, d), BF16), pltpu.VMEM((tm, d), F32)],
        compiler_params=_params(("parallel", "parallel", "arbitrary")),
        name="ffn",
    )(x, g.reshape(1, d), sh, sc, wg, wu, wd, g2, fg.reshape(1, d))


def _positional_features(n):
    pos = jnp.abs(jnp.arange(2 * n, dtype=F32) - n)
    t = pos[:, None] / n
    bands = jnp.linspace(1e-4, HY_BANDS - 1, HY_BANDS, dtype=F32)
    ang = 2.0 * math.pi * pos[:, None] * bands[None, :] / n
    return jnp.concatenate([t, jnp.cos(ang), -jnp.sin(ang)], axis=-1)


def kernel(x, c, ctx, c_ctx, ada_w, ada_b, norm1_g, w_in, conv_w, conv_b, hy_w1, hy_b1, hy_w2, hy_b2, hy_w3,
           hy_sin_freq, hy_decay, hy_bias, s5_lam_re, s5_lam_im, s5_log_dt, s5_b_re, s5_b_im, s5_c_re, s5_c_im,
           s5_d, s5_glu_w, s5_glu_b, branch_g_s5, branch_g_hy, w_out, norm2_g, ffn_w_gate, ffn_w_up,
           ffn_w_down, final_g):
    bsz, n_lat, d = x.shape
    n_ctx = ctx.shape[1]
    assert ada_w.shape[0] == 1, "single-layer block"
    l = 0
    s5_w = s5_glu_w.shape[1]
    hy_w = w_in.shape[2] - s5_w
    hy_w //= 3
    groups = s5_w // S5_GROUP
    assert n_lat % S5_CHUNK == 0 and n_ctx % S5_CHUNK == 0 and n_lat % GRID_W == 0

    pad_rows = -(bsz + 1) % 8
    cc = jnp.concatenate([c, c_ctx[None, :], jnp.zeros((pad_rows, d), F32)], axis=0)
    mod = _ada(cc, ada_w[l], ada_b[l])
    sh1, sc1, g1, sh2, sc2, g2 = [mod[:bsz, None, i * d:(i + 1) * d] for i in range(6)]
    csh1, csc1 = [jnp.broadcast_to(mod[bsz, None, None, i * d:(i + 1) * d], (bsz, 1, d)) for i in range(2)]

    w_in_b = w_in[l].astype(BF16)
    ua, ub, p = _inproj(x, norm1_g[l], sh1, sc1, w_in_b, conv_w[l], conv_b[l], s5_w, tm=min(512, n_lat))
    uca, ucb = _inproj(ctx, norm1_g[l], csh1, csc1, w_in_b[:, :s5_w], None, None, s5_w, tm=min(256, n_ctx))

    wb, wc, m, ar, ai = _s5_prep(s5_lam_re[l], s5_lam_im[l], s5_log_dt[l], s5_b_re[l], s5_b_im[l],
                                 s5_c_re[l], s5_c_im[l], s5_d[l])
    flat = lambda u: u.reshape(u.shape[0], -1, 128)
    ys = _s5(flat(ua), flat(ub), flat(uca), flat(ucb), wb, wc, m, ar, ai,
             n_ctx // S5_CHUNK, n_lat // S5_CHUNK, bsz)

    z = _positional_features(n_lat)
    e_pad = -z.shape[1] % 128
    z = jnp.pad(z, ((0, 0), (0, e_pad)))
    w1 = jnp.pad(hy_w1[l], ((0, e_pad), (0, 0)))
    taps = _filters(z, w1, hy_b1[l], hy_w2[l], hy_b2[l], hy_w3[l], hy_sin_freq[l],
                    hy_decay[l].reshape(-1), hy_w)
    assert n_lat % (HY_BLOCKS * 128) == 0
    cmat, smat = _dft_tables(n_lat // HY_BLOCKS)
    hr, hi, hn = _spectrum(taps, cmat, smat)
    bias = hy_bias[l].reshape(HY_ORDER, 1, hy_w)
    yh = _hconv(p, bias, hr, hi, hn, cmat, smat, hy_w)

    x1 = _mix(ys, yh, x, s5_glu_w[l].astype(BF16), s5_glu_b[l], branch_g_s5[l], branch_g_hy[l],
              w_out[l].astype(BF16), g1)

    return _ffn(x1, norm2_g[l], sh2, sc2, ffn_w_gate[l].astype(BF16), ffn_w_up[l].astype(BF16),
                ffn_w_down[l].astype(BF16), g2, final_g)
```

```python
import functools
import math

import numpy as np
import jax
import jax.numpy as jnp
from jax import lax
from jax.experimental import pallas as pl
from jax.experimental.pallas import tpu as pltpu

EPS = 1e-6
GRID_W = 64
S5_GROUP = 16
S5_STATE = 64
S5_CHUNK = 16
HY_ORDER = 2
HY_BANDS = 16
V7X_VMEM_BYTES = 64 * 1024 * 1024
VMEM_LIMIT = V7X_VMEM_BYTES - 8 * 1024 * 1024

F32 = jnp.float32
BF16 = jnp.bfloat16
HIGHEST = lax.Precision.HIGHEST


def _params(sem):
    return pltpu.CompilerParams(dimension_semantics=sem, vmem_limit_bytes=VMEM_LIMIT)


def _rms(x, g):
    return x * lax.rsqrt(jnp.mean(x * x, axis=-1, keepdims=True) + EPS) * g


ROW_BLOCK = 16


def _by_rows(n_rows, fn):
    for r in range(0, n_rows, ROW_BLOCK):
        fn(slice(r, r + ROW_BLOCK))


def _norm_modulate_rows(x_ref, g_ref, sh_ref, sc_ref, h_ref):
    gain = g_ref[...] * (1.0 + sc_ref[0])
    shift = sh_ref[0]

    def block(rows):
        h_ref[rows, :] = (_rms(x_ref[0, rows, :], gain) + shift).astype(BF16)

    _by_rows(h_ref.shape[0], block)


def _split_bf16(v):
    hi = v.astype(BF16)
    return hi, (v - hi.astype(F32)).astype(BF16)


def _dot_3pass(a, b):
    a_hi, a_lo = _split_bf16(a)
    b_hi, b_lo = _split_bf16(b)
    return (jnp.dot(a_hi, b_hi, preferred_element_type=F32) + jnp.dot(a_hi, b_lo, preferred_element_type=F32)
            + jnp.dot(a_lo, b_hi, preferred_element_type=F32))


def _ada_kernel(c_ref, w_ref, b_ref, o_ref):
    cv = c_ref[...]
    o_ref[...] = _dot_3pass(cv * jax.nn.sigmoid(cv), w_ref[...]) + b_ref[...]


def _ada(cc, w, b, tn=2048):
    r, d = cc.shape
    n = w.shape[1]
    return pl.pallas_call(
        _ada_kernel,
        grid=(n // tn,),
        in_specs=[pl.BlockSpec((r, d), lambda j: (0, 0)),
                  pl.BlockSpec((d, tn), lambda j: (0, j)),
                  pl.BlockSpec((1, tn), lambda j: (0, j))],
        out_specs=pl.BlockSpec((r, tn), lambda j: (0, j)),
        out_shape=jax.ShapeDtypeStruct((r, n), F32),
        compiler_params=_params(("arbitrary",)),
        name="ada",
    )(cc, w, b.reshape(1, n))


def _gran_transpose(a):
    r = lax.broadcasted_iota(jnp.int32, a.shape, 1)
    q = lax.broadcasted_iota(jnp.int32, a.shape, 2) // S5_GROUP
    for dist in (4, 2, 1):
        rb = (r & dist) != 0
        qb = (q & dist) != 0
        up = pltpu.roll(pltpu.roll(a, 8 - dist, axis=1), S5_GROUP * dist, axis=2)
        dn = pltpu.roll(pltpu.roll(a, dist, axis=1), 128 - S5_GROUP * dist, axis=2)
        a = jnp.where(rb == qb, a, jnp.where(rb, dn, up))
    return a


def _inproj_kernel(*refs, s5_cols, tn, conv):
    if conv:
        x_ref, g_ref, sh_ref, sc_ref, w_ref, cw_ref, cb_ref, ua_ref, ub_ref, o_ref, h_scr = refs
    else:
        x_ref, g_ref, sh_ref, sc_ref, w_ref, ua_ref, ub_ref, h_scr = refs
    tm = x_ref.shape[1]
    _norm_modulate_rows(x_ref, g_ref, sh_ref, sc_ref, h_scr)

    for jn in range(s5_cols // tn):
        acc = jnp.dot(h_scr[...], w_ref[:, jn * tn:(jn + 1) * tn], preferred_element_type=F32)
        for cb in range(tn // 128):
            tile = acc[:, cb * 128:(cb + 1) * 128].reshape(tm // S5_CHUNK, 2, 8, 128)
            gb = jn * (tn // 128) + cb
            ua_ref[gb, :, 0] = _gran_transpose(tile[:, 0])
            ub_ref[gb, :, 0] = _gran_transpose(tile[:, 1])

    if conv:
        col = lax.broadcasted_iota(jnp.int32, (GRID_W, tn), 0)
        for jn in range((w_ref.shape[1] - s5_cols) // tn):
            c0 = jn * tn
            acc = jnp.dot(h_scr[...], w_ref[:, s5_cols + c0:s5_cols + c0 + tn], preferred_element_type=F32)
            cw = cw_ref[:, c0:c0 + tn]
            bias = cb_ref[:, c0:c0 + tn]
            for r0 in range(0, tm, GRID_W):
                row = acc[r0:r0 + GRID_W]
                prev = jnp.where(col == 0, 0.0, pltpu.roll(row, 1, axis=0))
                nxt = jnp.where(col == GRID_W - 1, 0.0, pltpu.roll(row, GRID_W - 1, axis=0))
                o_ref[0, r0:r0 + GRID_W, c0:c0 + tn] = (prev * cw[0:1] + row * cw[1:2] + nxt * cw[2:3]
                                                        + bias).astype(o_ref.dtype)


def _inproj(x, g, sh, sc, w, cw, cb, s5_cols, tm, tn=512):
    bsz, n, d = x.shape
    nn = w.shape[1]
    conv = cw is not None
    assert tm % GRID_W == 0 and n % tm == 0 and s5_cols % tn == 0 and (nn - s5_cols) % tn == 0
    ngb = s5_cols // 128
    const = lambda shape: pl.BlockSpec(shape, lambda b, i: (0,) * len(shape), pipeline_mode=pl.Buffered(1))
    per_b = pl.BlockSpec((1, 1, d), lambda b, i: (b, 0, 0))
    in_specs = [pl.BlockSpec((1, tm, d), lambda b, i: (b, i, 0)), const((1, d)), per_b, per_b, const((d, nn))]
    args = [x, g.reshape(1, d), sh, sc, w]
    u_spec = pl.BlockSpec((ngb, tm // S5_CHUNK, 1, 8, 128), lambda b, i: (0, i, b, 0, 0))
    u_shape = jax.ShapeDtypeStruct((ngb, n // S5_CHUNK, bsz, 8, 128), F32)
    out_specs, out_shape = [u_spec, u_spec], [u_shape, u_shape]
    if conv:
        in_specs += [const((3, nn - s5_cols)), const((1, nn - s5_cols))]
        args += [cw, cb.reshape(1, -1)]
        out_specs.append(pl.BlockSpec((1, tm, nn - s5_cols), lambda b, i: (b, i, 0)))
        out_shape.append(jax.ShapeDtypeStruct((bsz, n, nn - s5_cols), BF16))
    return pl.pallas_call(
        functools.partial(_inproj_kernel, s5_cols=s5_cols, tn=tn, conv=conv),
        grid=(bsz, n // tm),
        in_specs=in_specs,
        out_specs=out_specs,
        out_shape=out_shape,
        scratch_shapes=[pltpu.VMEM((tm, d), BF16)],
        compiler_params=_params(("parallel", "parallel")),
        name="inproj" if conv else "inproj_ctx",
    )(*args)


def _s5_prep(lam_re, lam_im, log_dt, b_re, b_im, c_re, c_im, d):
    t = S5_CHUNK
    lam_re, lam_im = lam_re.astype(F32), lam_im.astype(F32)
    dt = jnp.exp(log_dt.astype(F32))[..., None]
    k = jnp.arange(t + 1, dtype=F32)[:, None, None, None]
    mag = jnp.exp(lam_re * dt * k)
    pr, pi = mag * jnp.cos(lam_im * dt * k), mag * jnp.sin(lam_im * dt * k)
    er, ei = pr[1] - 1.0, pi[1]
    den = lam_re * lam_re + lam_im * lam_im
    qr, qi = (er * lam_re + ei * lam_im) / den, (ei * lam_re - er * lam_im) / den
    b_re, b_im = b_re.astype(F32), b_im.astype(F32)
    bbr = qr[..., None] * b_re - qi[..., None] * b_im
    bbi = qr[..., None] * b_im + qi[..., None] * b_re
    cr, ci = c_re.astype(F32), c_im.astype(F32)
    g = lam_re.shape[1]

    def times_b(powr, powi, dr):
        ar_, ai_ = powr.transpose(1, 0, 2)[:, :, None, :], powi.transpose(1, 0, 2)[:, :, None, :]
        br_, bi_ = bbr[dr].transpose(0, 2, 1)[:, None], bbi[dr].transpose(0, 2, 1)[:, None]
        return ar_ * br_ - ai_ * bi_, ar_ * bi_ + ai_ * br_

    wbf_r, wbf_i = times_b(pr[t - 1::-1, 0], pi[t - 1::-1, 0], 0)
    wbb_r, wbb_i = times_b(pr[:t, 1], pi[:t, 1], 1)
    wb = jnp.concatenate([wbf_r, wbb_r, wbf_i, wbb_i], axis=-1).reshape(g, t * S5_GROUP, 4 * S5_STATE)

    def c_times(powr, powi, dr):
        cr_, ci_ = cr[dr].transpose(0, 2, 1)[:, :, None, :], ci[dr].transpose(0, 2, 1)[:, :, None, :]
        ar_, ai_ = powr.transpose(1, 2, 0)[..., None], powi.transpose(1, 2, 0)[..., None]
        return cr_ * ar_ - ci_ * ai_, cr_ * ai_ + ci_ * ar_

    wcf_r, wcf_i = c_times(pr[1:, 0], pi[1:, 0], 0)
    wcb_r, wcb_i = c_times(pr[t:0:-1, 1], pi[t:0:-1, 1], 1)
    wc = jnp.concatenate([wcf_r, wcb_r, -wcf_i, -wcb_i], axis=1).reshape(g, 4 * S5_STATE, t * S5_GROUP)

    def lag_kernel(dr):
        abr = pr[:t, dr, :, :, None] * bbr[dr] - pi[:t, dr, :, :, None] * bbi[dr]
        abi = pr[:t, dr, :, :, None] * bbi[dr] + pi[:t, dr, :, :, None] * bbr[dr]
        return (jnp.einsum('kgpe,gcp->kgec', abr, cr[dr], precision=HIGHEST)
                - jnp.einsum('kgpe,gcp->kgec', abi, ci[dr], precision=HIGHEST))

    kf, kb = lag_kernel(0), lag_kernel(1)
    kf = kf.at[0].add(jnp.eye(S5_GROUP, dtype=F32) * d.astype(F32)[:, None, :])
    strip = jnp.concatenate([kb[:0:-1], (kf[0] + kb[0])[None], kf[1:]], axis=0)
    strip = strip.transpose(1, 2, 0, 3).reshape(g, S5_GROUP, (2 * t - 1) * S5_GROUP)
    m = jnp.pad(strip, ((0, 0), (0, 0), (0, S5_GROUP)))

    ar = jnp.concatenate([pr[t, 0], pr[t, 1]], axis=-1)[:, None, :]
    ai = jnp.concatenate([pi[t, 0], pi[t, 1]], axis=-1)[:, None, :]
    return wb.astype(BF16), wc.astype(BF16), m, ar, ai


def _gelu_tanh(x):
    return 0.5 * x * (1.0 + jnp.tanh(math.sqrt(2.0 / math.pi) * (x + 0.044715 * (x * x * x))))


S5_GB = 8
S5_PASS = 4


def _s5_kernel(ua_ref, ub_ref, uca_ref, ucb_ref, wb_ref, wc_ref, m_ref, ar_ref, ai_ref, o_ref,
               p_re, p_im, fw_re, fw_im, bw_re, bw_im, y_a, y_b, m_scr, u_scr, *, n_ctx, n_lat, bsz):
    half = 2 * S5_STATE
    r_ctx, r_lat = n_ctx * bsz, n_lat * bsz
    is_fwd = lax.broadcasted_iota(jnp.int32, (bsz, half), 1) < S5_STATE
    sel = lax.broadcasted_iota(jnp.int32, (r_lat, half), 1) < S5_STATE

    def load_u(a_ref, b_ref, g8, r):
        return jnp.concatenate([a_ref[0, pl.ds(g8, r, stride=S5_GB), :],
                                b_ref[0, pl.ds(g8, r, stride=S5_GB), :]], axis=1).astype(BF16)

    def rows(i):
        return pl.ds(pl.multiple_of(i * bsz, bsz), bsz)

    def one_pass(pass_idx, _):
        first = pass_idx * S5_PASS
        for q in range(S5_PASS):
            wbg = wb_ref[first + q]
            pc = jnp.dot(load_u(uca_ref, ucb_ref, first + q, r_ctx), wbg, preferred_element_type=F32)
            u_scr[q] = load_u(ua_ref, ub_ref, first + q, r_lat)
            pn = jnp.dot(u_scr[q], wbg, preferred_element_type=F32)
            p_re[q, 0:r_ctx, :] = pc[:, :half]
            p_im[q, 0:r_ctx, :] = pc[:, half:]
            p_re[q, r_ctx:, :] = pn[:, :half]
            p_im[q, r_ctx:, :] = pn[:, half:]

        ars = [jnp.broadcast_to(ar_ref[first + q], (bsz, half)) for q in range(S5_PASS)]
        ais = [jnp.broadcast_to(ai_ref[first + q], (bsz, half)) for q in range(S5_PASS)]

        def advance(carry, fi, bi):
            out = []
            for q in range(S5_PASS):
                s_re, s_im = carry[2 * q], carry[2 * q + 1]
                in_re = jnp.where(is_fwd, p_re[q, rows(fi), :], p_re[q, rows(bi), :])
                in_im = jnp.where(is_fwd, p_im[q, rows(fi), :], p_im[q, rows(bi), :])
                out += [ars[q] * s_re - ais[q] * s_im + in_re, ars[q] * s_im + ais[q] * s_re + in_im]
            return tuple(out)

        def ctx_step(k, carry):
            return advance(carry, k, n_ctx - 1 - k)

        def lat_step(k, carry):
            kb = n_lat - 1 - k
            for q in range(S5_PASS):
                fw_re[q, rows(k), :] = carry[2 * q]
                fw_im[q, rows(k), :] = carry[2 * q + 1]
                bw_re[q, rows(kb), :] = carry[2 * q]
                bw_im[q, rows(kb), :] = carry[2 * q + 1]
            return advance(carry, n_ctx + k, n_ctx + kb)

        carry = lax.fori_loop(0, n_ctx, ctx_step, (jnp.zeros((bsz, half), F32),) * (2 * S5_PASS))
        lax.fori_loop(0, n_lat, lat_step, carry)

        for q in range(S5_PASS):
            g8 = first + q
            ent = jnp.concatenate([jnp.where(sel, fw_re[q], bw_re[q]),
                                   jnp.where(sel, fw_im[q], bw_im[q])], axis=1).astype(BF16)
            strip = m_ref[g8]
            for sig in range(S5_CHUNK):
                lo = (S5_CHUNK - 1 - sig) * S5_GROUP
                m_scr[sig * S5_GROUP:(sig + 1) * S5_GROUP, :] = strip[:, lo:lo + S5_CHUNK * S5_GROUP].astype(BF16)
            y = (jnp.dot(u_scr[q], m_scr[...], preferred_element_type=F32)
                 + jnp.dot(ent, wc_ref[g8], preferred_element_type=F32))
            y = _gelu_tanh(y)
            y_a[pl.ds(g8, r_lat, stride=S5_GB), :] = y[:, :128]
            y_b[pl.ds(g8, r_lat, stride=S5_GB), :] = y[:, 128:]
        return 0

    lax.fori_loop(0, S5_GB // S5_PASS, one_pass, 0)

    cpb = math.gcd(n_lat, 16)

    def relayout(i, _):
        src = pl.ds(pl.multiple_of(i * (cpb * bsz * S5_GB), cpb * bsz * S5_GB), cpb * bsz * S5_GB)
        dst = pl.ds(pl.multiple_of(i * (cpb * S5_CHUNK), cpb * S5_CHUNK), cpb * S5_CHUNK)
        t_a = _gran_transpose(y_a[src, :].reshape(cpb * bsz, 8, 128)).reshape(cpb, bsz, 8, 128)
        t_b = _gran_transpose(y_b[src, :].reshape(cpb * bsz, 8, 128)).reshape(cpb, bsz, 8, 128)
        for b in range(bsz):
            tok = jnp.concatenate([t_a[:, b][:, None], t_b[:, b][:, None]], axis=1)
            o_ref[b, dst, :] = tok.reshape(cpb * S5_CHUNK, 128).astype(o_ref.dtype)
        return 0

    lax.fori_loop(0, n_lat // cpb, relayout, 0)


def _s5(ua, ub, uca, ucb, wb, wc, m, ar, ai, n_ctx, n_lat, bsz):
    ngb = ua.shape[0]
    half = 2 * S5_STATE
    w = S5_CHUNK * S5_GROUP
    r_ctx, r_lat = n_ctx * bsz, n_lat * bsz
    u_blk = lambda r: pl.BlockSpec((1, r * S5_GB, 128), lambda i: (i, 0, 0))
    w_blk = lambda shape: pl.BlockSpec((S5_GB,) + shape, lambda i: (i, 0, 0))
    pass_scr = lambda r: pltpu.VMEM((S5_PASS, r, half), F32)
    return pl.pallas_call(
        functools.partial(_s5_kernel, n_ctx=n_ctx, n_lat=n_lat, bsz=bsz),
        grid=(ngb,),
        in_specs=[u_blk(r_lat), u_blk(r_lat), u_blk(r_ctx), u_blk(r_ctx),
                  w_blk((w, 2 * half)), w_blk((2 * half, w)), w_blk(m.shape[1:]), w_blk((1, half)), w_blk((1, half))],
        out_specs=pl.BlockSpec((bsz, n_lat * S5_CHUNK, 128), lambda i: (0, 0, i)),
        out_shape=jax.ShapeDtypeStruct((bsz, n_lat * S5_CHUNK, ngb * 128), BF16),
        scratch_shapes=[pass_scr(r_ctx + r_lat)] * 2 + [pass_scr(r_lat)] * 4
                       + [pltpu.VMEM((r_lat * S5_GB, 128), F32)] * 2
                       + [pltpu.VMEM((w, w), BF16), pltpu.VMEM((S5_PASS, r_lat, w), BF16)],
        compiler_params=_params(("parallel",)),
        name="s5",
    )(ua, ub, uca, ucb, wb, wc, m, ar, ai)


def _filt_kernel(z_ref, w1_ref, b1_ref, w2_ref, b2_ref, w3b_ref, w3f_ref, fr_ref, decb_ref, decf_ref, o_ref,
                 hi_scr, lo_scr):
    n = z_ref.shape[0]
    e = z_ref.shape[1] // 2
    f = hi_scr.shape[1] // 2

    @pl.when((pl.program_id(0) == 0) & (pl.program_id(1) == 0))
    def _():
        fr = fr_ref[...]
        h = jnp.sin(fr[0:1] * (jnp.dot(z_ref[...], w1_ref[...], precision=HIGHEST, preferred_element_type=F32)
                               + b1_ref[...]))
        h = jnp.sin(fr[1:2] * (jnp.dot(h, w2_ref[...], precision=HIGHEST, preferred_element_type=F32)
                               + b2_ref[...]))
        hi_scr[...], lo_scr[...] = _split_bf16(h)

    def last_layer(cols, w_ref):
        w_hi, w_lo = _split_bf16(w_ref[...])
        h_hi = hi_scr[:, cols]
        return (jnp.dot(h_hi, w_hi, preferred_element_type=F32) + jnp.dot(h_hi, w_lo, preferred_element_type=F32)
                + jnp.dot(lo_scr[:, cols], w_hi, preferred_element_type=F32))

    bwd = last_layer(slice(0, f), w3b_ref) * jnp.exp(-z_ref[:, 0:1] * jnp.abs(decb_ref[...]))
    bwd = jnp.where(lax.broadcasted_iota(jnp.int32, bwd.shape, 0) == 0, 0.0, bwd)
    fwd = last_layer(slice(f, 2 * f), w3f_ref) * jnp.exp(-z_ref[:, e:e + 1] * jnp.abs(decf_ref[...]))
    den = jnp.sum(jnp.abs(bwd), axis=0, keepdims=True) + jnp.sum(jnp.abs(fwd), axis=0, keepdims=True) + EPS
    o_ref[0, 0:n, :] = bwd / den
    o_ref[0, n:, :] = fwd / den


def _filters(z2, w1, b1, w2, b2, w3, freq, decay, hy_w, tn=512):
    n2, e = z2.shape
    n = n2 // 2
    f = w1.shape[1]
    nt = hy_w // tn
    zz = jnp.concatenate([z2[:n], z2[n:]], axis=1)
    twice = lambda w: jnp.concatenate([jnp.pad(w, ((0, 0), (0, w.shape[1]))), jnp.pad(w, ((0, 0), (w.shape[1], 0)))])
    pair = lambda v: jnp.concatenate([v, v], axis=-1)
    full = lambda shape: pl.BlockSpec(shape, lambda o, j: (0, 0))
    fwd_col = lambda r: pl.BlockSpec((r, tn), lambda o, j: (0, o * 2 * nt + j))
    bwd_col = lambda r: pl.BlockSpec((r, tn), lambda o, j: (0, o * 2 * nt + nt + j))
    dec = decay.reshape(1, -1)
    return pl.pallas_call(
        _filt_kernel,
        grid=(HY_ORDER, nt),
        in_specs=[full((n, 2 * e)), full((2 * e, 2 * f)), full((1, 2 * f)), full((2 * f, 2 * f)), full((1, 2 * f)),
                  bwd_col(f), fwd_col(f), full((2, 2 * f)), bwd_col(1), fwd_col(1)],
        out_specs=pl.BlockSpec((1, n2, tn), lambda o, j: (o, 0, j)),
        out_shape=jax.ShapeDtypeStruct((HY_ORDER, n2, hy_w), F32),
        scratch_shapes=[pltpu.VMEM((n, 2 * f), BF16)] * 2,
        compiler_params=_params(("arbitrary", "arbitrary")),
        name="filt",
    )(zz, twice(w1), pair(b1.reshape(1, f)), twice(w2), pair(b2.reshape(1, f)), w3, w3, pair(freq), dec, dec)


HY_BLOCKS = 4


def _dft_tables(n):
    k = np.arange(n, dtype=np.int64)
    ang = (np.outer(k, k) % (2 * n)).astype(np.float64) * (math.pi / n)
    return jnp.asarray(np.cos(ang), dtype=F32).astype(BF16), jnp.asarray(-np.sin(ang), dtype=F32).astype(BF16)


def _alt(n):
    return jnp.where(lax.broadcasted_iota(jnp.int32, (n, 1), 0) % 2 == 0, 1.0, -1.0).astype(F32)


def _spec_kernel(h_ref, c_ref, s_ref, hr_ref, hi_ref, hn_ref):
    m = c_ref.shape[0]
    cmat = c_ref[...]
    smat = s_ref[...]
    alt = _alt(m)
    wk = jnp.where(lax.broadcasted_iota(jnp.int32, (m, 1), 0) == 0, 0.5 / m, 1.0 / m)
    prev = None
    for j in range(2 * HY_BLOCKS):
        blk = h_ref[0, j * m:(j + 1) * m, :]
        bb = blk.astype(BF16)
        cur = (jnp.dot(cmat, bb, preferred_element_type=F32), jnp.dot(smat, bb, preferred_element_type=F32),
               jnp.sum(blk * alt, axis=0, keepdims=True), blk[0:1])
        if prev is not None:
            hr_ref[0, j - 1] = ((cur[0] + alt * (prev[0] - prev[3])) * wk).astype(hr_ref.dtype)
            hi_ref[0, j - 1] = ((cur[1] + alt * prev[1]) * wk).astype(hi_ref.dtype)
            hn_ref[0, j - 1] = (cur[2] + prev[2] - prev[3]) * (0.5 / m)
        prev = cur


def _spectrum(taps, cmat, smat, tn=256):
    _, n2, hy_w = taps.shape
    m = cmat.shape[0]
    nd = 2 * HY_BLOCKS - 1
    const = lambda shape: pl.BlockSpec(shape, lambda o, j: (0, 0), pipeline_mode=pl.Buffered(1))
    out_blk = lambda r: pl.BlockSpec((1, nd, r, tn), lambda o, j: (o, 0, 0, j))
    return pl.pallas_call(
        _spec_kernel,
        grid=(HY_ORDER, hy_w // tn),
        in_specs=[pl.BlockSpec((1, n2, tn), lambda o, j: (o, 0, j)), const((m, m)), const((m, m))],
        out_specs=[out_blk(m), out_blk(m), out_blk(1)],
        out_shape=[jax.ShapeDtypeStruct((HY_ORDER, nd, m, hy_w), BF16),
                   jax.ShapeDtypeStruct((HY_ORDER, nd, m, hy_w), BF16),
                   jax.ShapeDtypeStruct((HY_ORDER, nd, 1, hy_w), F32)],
        compiler_params=_params(("parallel", "parallel")),
        name="spec",
    )(taps, cmat, smat)


def _hconv_kernel(v_ref, x1_ref, x2_ref, bias_ref, hr_ref, hi_ref, hn_ref, c_ref, s_ref, o_ref, z1_scr):
    m = c_ref.shape[0]
    cmat = c_ref[...]
    smat = s_ref[...]
    alt = _alt(m)

    def gated_conv(order, read_z, read_gate, write):
        zr, zi, zn = [], [], []
        for j in range(HY_BLOCKS):
            zj = read_z(slice(j * m, (j + 1) * m))
            zr.append(jnp.dot(cmat, zj, preferred_element_type=F32).astype(BF16))
            zi.append(jnp.dot(smat, zj, preferred_element_type=F32).astype(BF16))
            zn.append(jnp.sum(zj.astype(F32) * alt, axis=0, keepdims=True))
        for i in range(HY_BLOCKS):
            yr = yi = yn = None
            for j in range(HY_BLOCKS):
                d = i - j + HY_BLOCKS - 1
                hr = hr_ref[order, d]
                hi = hi_ref[order, d]
                pr = zr[j] * hr - zi[j] * hi
                pi = zr[j] * hi + zi[j] * hr
                pn = zn[j] * hn_ref[order, d]
                yr, yi, yn = (pr, pi, pn) if yr is None else (yr + pr, yi + pi, yn + pn)
            rows = slice(i * m, (i + 1) * m)
            y = (jnp.dot(cmat, yr, preferred_element_type=F32) + jnp.dot(smat, yi, preferred_element_type=F32)
                 + alt * yn + bias_ref[order] * read_z(rows).astype(F32))
            write(rows, (read_gate(rows).astype(F32) * y).astype(BF16))

    def write_z1(rows, val):
        z1_scr[rows, :] = val

    def write_out(rows, val):
        o_ref[0, rows, :] = val

    gated_conv(0, lambda r: v_ref[0, r, :], lambda r: x1_ref[0, r, :], write_z1)
    gated_conv(1, lambda r: z1_scr[r, :], lambda r: x2_ref[0, r, :], write_out)


def _hconv(hs, bias, hr, hi, hn, cmat, smat, hy_w, tc=256):
    assert HY_ORDER == 2
    bsz, n, _ = hs.shape
    m = cmat.shape[0]
    nd = 2 * HY_BLOCKS - 1
    nt = hy_w // tc
    const = lambda shape: pl.BlockSpec(shape, lambda j, b: (0, 0), pipeline_mode=pl.Buffered(1))
    spec_blk = lambda r: pl.BlockSpec((HY_ORDER, nd, r, tc), lambda j, b: (0, 0, 0, j))
    window = lambda k: pl.BlockSpec((1, n, tc), lambda j, b: (b, 0, k * nt + j))
    return pl.pallas_call(
        _hconv_kernel,
        grid=(nt, bsz),
        in_specs=[window(0), window(1), window(2),
                  pl.BlockSpec((HY_ORDER, 1, tc), lambda j, b: (0, 0, j)),
                  spec_blk(m), spec_blk(m), spec_blk(1),
                  const((m, m)), const((m, m))],
        out_specs=pl.BlockSpec((1, n, tc), lambda j, b: (b, 0, j)),
        out_shape=jax.ShapeDtypeStruct((bsz, n, hy_w), BF16),
        scratch_shapes=[pltpu.VMEM((n, tc), BF16)],
        compiler_params=_params(("parallel", "parallel")),
        name="hconv",
    )(hs, hs, hs, bias, hr, hi, hn, cmat, smat)


def _mix_kernel(ys_ref, yh_ref, x_ref, gw_ref, gb_ref, gs_ref, gh_ref, wo_ref, g1_ref, o_ref, *, s5_w):
    gl = jnp.dot(ys_ref[0], gw_ref[...], preferred_element_type=F32) + gb_ref[...]
    a = gl[:, :s5_w] * jax.nn.sigmoid(gl[:, s5_w:])
    a = _rms(a, gs_ref[...]).astype(BF16)
    yh = _rms(yh_ref[0].astype(F32), gh_ref[...]).astype(BF16)
    proj = (jnp.dot(a, wo_ref[:s5_w, :], preferred_element_type=F32)
            + jnp.dot(yh, wo_ref[s5_w:, :], preferred_element_type=F32))
    o_ref[0] = x_ref[0] + g1_ref[0] * proj


def _mix(ys, yh, x, glu_w, glu_b, g_s5, g_hy, w_out, g1, tm=512):
    bsz, n, d = x.shape
    tm = min(tm, n)
    s5_w = ys.shape[-1]
    hy_w = yh.shape[-1]
    const = lambda shape: pl.BlockSpec(shape, lambda b, i: (0, 0), pipeline_mode=pl.Buffered(1))
    tok = lambda w: pl.BlockSpec((1, tm, w), lambda b, i: (b, i, 0))
    return pl.pallas_call(
        functools.partial(_mix_kernel, s5_w=s5_w),
        grid=(bsz, n // tm),
        in_specs=[tok(s5_w), tok(hy_w), tok(d),
                  const((s5_w, 2 * s5_w)), const((1, 2 * s5_w)), const((1, s5_w)), const((1, hy_w)),
                  const((s5_w + hy_w, d)),
                  pl.BlockSpec((1, 1, d), lambda b, i: (b, 0, 0))],
        out_specs=tok(d),
        out_shape=jax.ShapeDtypeStruct((bsz, n, d), F32),
        compiler_params=_params(("parallel", "parallel")),
        name="mix",
    )(ys, yh, x, glu_w, glu_b.reshape(1, -1), g_s5.reshape(1, -1), g_hy.reshape(1, -1), w_out, g1)


def _ffn_kernel(x_ref, g_ref, sh_ref, sc_ref, wg_ref, wu_ref, wd_ref, g2_ref, fg_ref, o_ref, h_scr, acc_scr):
    j = pl.program_id(2)

    @pl.when(j == 0)
    def _():
        _norm_modulate_rows(x_ref, g_ref, sh_ref, sc_ref, h_scr)
        acc_scr[...] = jnp.zeros_like(acc_scr)

    h = h_scr[...]
    half = wg_ref.shape[1] // 2
    down = None
    for c0 in (0, half):
        gate = jnp.dot(h, wg_ref[:, c0:c0 + half], preferred_element_type=F32)
        up = jnp.dot(h, wu_ref[:, c0:c0 + half], preferred_element_type=F32)
        act = (gate * jax.nn.sigmoid(gate) * up).astype(BF16)
        part = jnp.dot(act, wd_ref[c0:c0 + half, :], preferred_element_type=F32)
        down = part if down is None else down + part
    acc_scr[...] += down

    @pl.when(j == pl.num_programs(2) - 1)
    def _():
        gate2 = g2_ref[0]
        final_gain = fg_ref[...]

        def block(rows):
            o_ref[0, rows, :] = _rms(x_ref[0, rows, :] + gate2 * acc_scr[rows, :], final_gain)

        _by_rows(acc_scr.shape[0], block)


def _ffn(x, g, sh, sc, wg, wu, wd, g2, fg, tm=512, tf=512):
    bsz, n, d = x.shape
    tm = min(tm, n)
    dff = wg.shape[1]
    row = lambda: pl.BlockSpec((1, d), lambda b, i, j: (0, 0))
    per_b = lambda: pl.BlockSpec((1, 1, d), lambda b, i, j: (b, 0, 0))
    return pl.pallas_call(
        _ffn_kernel,
        grid=(bsz, n // tm, dff // tf),
        in_specs=[pl.BlockSpec((1, tm, d), lambda b, i, j: (b, i, 0)),
                  row(), per_b(), per_b(),
                  pl.BlockSpec((d, tf), lambda b, i, j: (0, j)),
                  pl.BlockSpec((d, tf), lambda b, i, j: (0, j)),
                  pl.BlockSpec((tf, d), lambda b, i, j: (j, 0)),
                  per_b(), row()],
        out_specs=pl.BlockSpec((1, tm, d), lambda b, i, j: (b, i, 0)),
        out_shape=jax.ShapeDtypeStruct((bsz, n, d), F32),
        scratch_shapes=[pltpu.VMEM((tm, d), BF16), pltpu.VMEM((tm, d), F32)],
        compiler_params=_params(("parallel", "parallel", "arbitrary")),
        name="ffn",
    )(x, g.reshape(1, d), sh, sc, wg, wu, wd, g2, fg.reshape(1, d))


def _positional_features(n):
    pos = jnp.abs(jnp.arange(2 * n, dtype=F32) - n)
    t = pos[:, None] / n
    bands = jnp.linspace(1e-4, HY_BANDS - 1, HY_BANDS, dtype=F32)
    ang = 2.0 * math.pi * pos[:, None] * bands[None, :] / n
    return jnp.concatenate([t, jnp.cos(ang), -jnp.sin(ang)], axis=-1)


def kernel(x, c, ctx, c_ctx, ada_w, ada_b, norm1_g, w_in, conv_w, conv_b, hy_w1, hy_b1, hy_w2, hy_b2, hy_w3,
           hy_sin_freq, hy_decay, hy_bias, s5_lam_re, s5_lam_im, s5_log_dt, s5_b_re, s5_b_im, s5_c_re, s5_c_im,
           s5_d, s5_glu_w, s5_glu_b, branch_g_s5, branch_g_hy, w_out, norm2_g, ffn_w_gate, ffn_w_up,
           ffn_w_down, final_g):
    bsz, n_lat, d = x.shape
    n_ctx = ctx.shape[1]
    assert ada_w.shape[0] == 1, "single-layer block"
    l = 0
    s5_w = s5_glu_w.shape[1]
    hy_w = (w_in.shape[2] - s5_w) // 3
    assert s5_w % (S5_GB * S5_GROUP) == 0 and bsz == 8, "8 batch rows fill the sublanes of the chunk scan"
    assert n_lat % S5_CHUNK == 0 and n_ctx % S5_CHUNK == 0 and n_lat % GRID_W == 0

    pad_rows = -(bsz + 1) % 8
    cc = jnp.concatenate([c, c_ctx[None, :], jnp.zeros((pad_rows, d), F32)], axis=0)
    mod = _ada(cc, ada_w[l], ada_b[l])
    sh1, sc1, g1, sh2, sc2, g2 = [mod[:bsz, None, i * d:(i + 1) * d] for i in range(6)]
    csh1, csc1 = [jnp.broadcast_to(mod[bsz, None, None, i * d:(i + 1) * d], (bsz, 1, d)) for i in range(2)]

    w_in_b = w_in[l].astype(BF16)
    ua, ub, p = _inproj(x, norm1_g[l], sh1, sc1, w_in_b, conv_w[l], conv_b[l], s5_w, tm=min(512, n_lat))
    uca, ucb = _inproj(ctx, norm1_g[l], csh1, csc1, w_in_b[:, :s5_w], None, None, s5_w, tm=min(256, n_ctx))

    wb, wc, m, ar, ai = _s5_prep(s5_lam_re[l], s5_lam_im[l], s5_log_dt[l], s5_b_re[l], s5_b_im[l],
                                 s5_c_re[l], s5_c_im[l], s5_d[l])
    flat = lambda u: u.reshape(u.shape[0], -1, 128)
    ys = _s5(flat(ua), flat(ub), flat(uca), flat(ucb), wb, wc, m, ar, ai,
             n_ctx // S5_CHUNK, n_lat // S5_CHUNK, bsz)

    z = _positional_features(n_lat)
    e_pad = -z.shape[1] % 128
    z = jnp.pad(z, ((0, 0), (0, e_pad)))
    w1 = jnp.pad(hy_w1[l], ((0, e_pad), (0, 0)))
    taps = _filters(z, w1, hy_b1[l], hy_w2[l], hy_b2[l], hy_w3[l], hy_sin_freq[l],
                    hy_decay[l].reshape(-1), hy_w)
    assert n_lat % (HY_BLOCKS * 128) == 0
    cmat, smat = _dft_tables(n_lat // HY_BLOCKS)
    hr, hi, hn = _spectrum(taps, cmat, smat)
    bias = hy_bias[l].reshape(HY_ORDER, 1, hy_w)
    yh = _hconv(p, bias, hr, hi, hn, cmat, smat, hy_w)

    x1 = _mix(ys, yh, x, s5_glu_w[l].astype(BF16), s5_glu_b[l], branch_g_s5[l], branch_g_hy[l],
              w_out[l].astype(BF16), g1)

    return _ffn(x1, norm2_g[l], sh2, sc2, ffn_w_gate[l].astype(BF16), ffn_w_up[l].astype(BF16),
                ffn_w_down[l].astype(BF16), g2, final_g)
```
